```python
import jax, jax.numpy as jnp
from jax import lax
import numpy as np

D_MODEL = 2048
BATCH = 2
SEQ = 4096
DEPTH = 4

MIX_A = D_MODEL // 4
MIX_B = D_MODEL // 2
MIX_C = D_MODEL // 4
GM_GROUPS = 4
GM_GROUP_W = MIX_A // GM_GROUPS
CHUNK = 128
RWKV_HEAD = 64
RWKV_HEADS = MIX_B // RWKV_HEAD
DECAY_LORA = 96
ICLR_LORA = 96
GATE_LORA = 256
RWKV_COLS = 3 * MIX_B + DECAY_LORA + ICLR_LORA + GATE_LORA
CONV_K = 3
CONV_GROUPS = 8
CONV_GROUP_W = MIX_C // CONV_GROUPS
N_IN = 2 * MIX_A + RWKV_COLS + 3 * MIX_C
D_FF = 4 * D_MODEL
RMS_EPS = 1e-6
LNX_EPS = 64e-5

kernel_name = "hybrid_gmlp_rwkv7_shortconv_trunk"


def rms_norm(x, gain):
    xf = x.astype(jnp.float32)
    y = xf * lax.rsqrt(jnp.mean(jnp.square(xf), axis=-1, keepdims=True) + RMS_EPS)
    return (y * gain.astype(jnp.float32)).astype(x.dtype)


def gmlp_mix(p, v_gain, ws, bs):
    b_, t_, _ = p.shape
    z = jax.nn.gelu(p)
    u, v = jnp.split(z, 2, axis=-1)
    u = u.reshape(b_, t_, GM_GROUPS, GM_GROUP_W)
    v = rms_norm(v.reshape(b_, t_, GM_GROUPS, GM_GROUP_W), v_gain.reshape(GM_GROUPS, GM_GROUP_W))
    vc = v.reshape(b_, t_ // CHUNK, CHUNK, GM_GROUPS, GM_GROUP_W)
    causal = jnp.tril(jnp.ones((CHUNK, CHUNK), dtype=bool))
    w = jnp.where(causal[None], ws, 0)
    mixed = jnp.einsum('gts,bcsgd->bctgd', w, vc) + bs.T[None, None, :, :, None]
    return u * mixed.reshape(b_, t_, GM_GROUPS, GM_GROUP_W)


def token_shift(p, mu):
    prev = jnp.pad(p, ((0, 0), (1, 0), (0, 0)))[:, :-1]
    return p + (prev - p) * mu


def wkv7_scan(r, w, k, v, a, b):
    b_, t_, h_, n_ = r.shape
    xs = tuple(jnp.moveaxis(t.astype(jnp.float32), 1, 0) for t in (r, w, k, v, a, b))

    def step(S, inp):
        r_t, w_t, k_t, v_t, a_t, b_t = inp
        sa = jnp.einsum('bhij,bhj->bhi', S, a_t)
        S = S * w_t[:, :, None, :] + sa[..., :, None] * b_t[..., None, :] + v_t[..., :, None] * k_t[..., None, :]
        y = jnp.einsum('bhij,bhj->bhi', S, r_t)
        return S, y

    S0 = jnp.zeros((b_, h_, n_, n_), jnp.float32)
    _, ys = lax.scan(step, S0, xs)
    return jnp.moveaxis(ys, 0, 1)


def rwkv7_mix(p, mu, w0, w_up, a0, a_up, g_up, k_k, k_a, r_k, ln_w, ln_b):
    b_, t_, _ = p.shape
    p = token_shift(p, mu)
    idx = [MIX_B, 2 * MIX_B, 3 * MIX_B, 3 * MIX_B + DECAY_LORA, 3 * MIX_B + DECAY_LORA + ICLR_LORA]
    r, k, v, wd, ad, gd = jnp.split(p, idx, axis=-1)
    w = -jax.nn.softplus(-(w0 + jnp.tanh(wd) @ w_up)) - 0.5
    decay = jnp.exp(-jnp.exp(w.astype(jnp.float32)))
    a = jax.nn.sigmoid(a0 + ad @ a_up)
    g = jax.nn.sigmoid(gd) @ g_up
    heads = lambda t: t.reshape(b_, t_, RWKV_HEADS, RWKV_HEAD)
    kkf = heads(k * k_k).astype(jnp.float32)
    kk = kkf / jnp.maximum(jnp.sqrt(jnp.sum(jnp.square(kkf), axis=-1, keepdims=True)), 1e-12)
    k = k * (1 + (a - 1) * k_a)
    rh, kh, vh, ah = heads(r), heads(k), heads(v), heads(a)
    y = wkv7_scan(rh, heads(decay), kh, vh, -kk, kk * ah.astype(jnp.float32))
    mean = jnp.mean(y, axis=-1, keepdims=True)
    var = jnp.mean(jnp.square(y - mean), axis=-1, keepdims=True)
    y = (y - mean) * lax.rsqrt(var + LNX_EPS)
    y = (y * ln_w.reshape(RWKV_HEADS, RWKV_HEAD) + ln_b.reshape(RWKV_HEADS, RWKV_HEAD)).astype(p.dtype)
    bonus = jnp.sum(rh * kh * r_k, axis=-1, keepdims=True) * vh
    return ((y + bonus) * heads(g)).reshape(b_, t_, MIX_B)


def short_conv_mix(p, conv_w):
    gb, gc, h = jnp.split(p, 3, axis=-1)
    z = gc * h
    zc = lax.conv_general_dilated(z, conv_w[:, None, :], window_strides=(1,), padding=[(CONV_K - 1, 0)],
                                  dimension_numbers=('NWC', 'WIO', 'NWC'), feature_group_count=MIX_C)
    return gb * zc


def setup_inputs(seed: int = 0) -> dict:
    key = jax.random.key(seed)
    ks = iter(jax.random.split(key, 40))
    L = DEPTH
    f32 = jnp.float32

    def normal(shape, scale):
        return jax.random.normal(next(ks), shape, f32) * scale

    def gain(shape):
        return 1.0 + normal(shape, 0.02)

    def unif(shape, lo, hi):
        return jax.random.uniform(next(ks), shape, f32, lo, hi)

    return {
        "x": normal((BATCH, SEQ, D_MODEL), 1.0),
        "norm_mix_pre": gain((L, D_MODEL)),
        "norm_mix_post": gain((L, D_MODEL)),
        "norm_mlp_pre": gain((L, D_MODEL)),
        "norm_mlp_post": gain((L, D_MODEL)),
        "w_in": normal((L, D_MODEL, N_IN), D_MODEL ** -0.5),
        "gm_v_gain": gain((L, MIX_A)),
        "gm_ws": normal((L, GM_GROUPS, CHUNK, CHUNK), CHUNK ** -0.5),
        "gm_bs": gain((L, GM_GROUPS, CHUNK)),
        "gm_out_gain": gain((L, MIX_A)),
        "rk_mu": unif((L, RWKV_COLS), 0.0, 1.0),
        "rk_w0": unif((L, MIX_B), -5.0, -1.0),
        "rk_w_up": normal((L, DECAY_LORA, MIX_B), 0.5 * DECAY_LORA ** -0.5),
        "rk_a0": normal((L, MIX_B), 0.1),
        "rk_a_up": normal((L, ICLR_LORA, MIX_B), ICLR_LORA ** -0.5),
        "rk_g_up": normal((L, GATE_LORA, MIX_B), GATE_LORA ** -0.5),
        "rk_k_k": 1.0 + normal((L, MIX_B), 0.1),
        "rk_k_a": 1.0 + normal((L, MIX_B), 0.1),
        "rk_r_k": normal((L, RWKV_HEADS, RWKV_HEAD), 0.1),
        "rk_ln_w": gain((L, MIX_B)),
        "rk_ln_b": normal((L, MIX_B), 0.02),
        "sc_conv": normal((L, CONV_K, MIX_C), CONV_K ** -0.5),
        "sc_out_gain": gain((L, MIX_C)),
        "w_out": normal((L, D_MODEL, D_MODEL), D_MODEL ** -0.5),
        "mlp_up": normal((L, D_MODEL, D_FF), D_MODEL ** -0.5),
        "mlp_down": normal((L, D_FF, D_MODEL), D_FF ** -0.5),
    }


def reference(x, norm_mix_pre, norm_mix_post, norm_mlp_pre, norm_mlp_post, w_in, gm_v_gain, gm_ws, gm_bs,
              gm_out_gain, rk_mu, rk_w0, rk_w_up, rk_a0, rk_a_up, rk_g_up, rk_k_k, rk_k_a, rk_r_k, rk_ln_w,
              rk_ln_b, sc_conv, sc_out_gain, w_out, mlp_up, mlp_down):
    b_, t_, _ = x.shape
    for l in range(DEPTH):
        h = rms_norm(x, norm_mix_pre[l])
        p = h @ w_in[l]
        pa, pb, pc = jnp.split(p, [2 * MIX_A, 2 * MIX_A + RWKV_COLS], axis=-1)
        ya = rms_norm(gmlp_mix(pa, gm_v_gain[l], gm_ws[l], gm_bs[l]),
                      gm_out_gain[l].reshape(GM_GROUPS, GM_GROUP_W)).reshape(b_, t_, MIX_A)
        yb = rwkv7_mix(pb, rk_mu[l], rk_w0[l], rk_w_up[l], rk_a0[l], rk_a_up[l], rk_g_up[l], rk_k_k[l],
                       rk_k_a[l], rk_r_k[l], rk_ln_w[l], rk_ln_b[l])
        yc = rms_norm(short_conv_mix(pc, sc_conv[l]).reshape(b_, t_, CONV_GROUPS, CONV_GROUP_W),
                      sc_out_gain[l].reshape(CONV_GROUPS, CONV_GROUP_W)).reshape(b_, t_, MIX_C)
        y = jnp.concatenate([ya, yb, yc], axis=-1) @ w_out[l]
        x = x + rms_norm(y, norm_mix_post[l])
        h = rms_norm(x, norm_mlp_pre[l])
        f = jnp.square(jax.nn.relu(h @ mlp_up[l])) @ mlp_down[l]
        x = x + rms_norm(f, norm_mlp_post[l])
    return x
```

```python
import functools

import jax
import jax.numpy as jnp
from jax import lax
from jax.experimental import pallas as pl
from jax.experimental.pallas import tpu as pltpu

F32 = jnp.float32
BF16 = jnp.bfloat16
HIGHEST = lax.Precision.HIGHEST

RMS_EPS = 1e-6
LNX_EPS = 64e-5
RWKV_HEAD = 64
CONV_GROUP_W = 64
LANES = 128
SUBLANES = 8
WKV_CHUNK = 128
VMEM_LIMIT = 56 * 1024 * 1024


def _params(*sem):
    return pltpu.CompilerParams(dimension_semantics=sem, vmem_limit_bytes=VMEM_LIMIT)


def _rms(x, gain):
    return x * lax.rsqrt(jnp.mean(x * x, axis=-1, keepdims=True) + RMS_EPS) * gain


def _dot(a, b):
    return jnp.dot(a, b, preferred_element_type=F32)


def _dot_hi(a, b):
    return jnp.dot(a, b, preferred_element_type=F32, precision=HIGHEST)


def _dot_nt_hi(a, b):
    return lax.dot_general(a, b, (((1,), (1,)), ((), ())), preferred_element_type=F32, precision=HIGHEST)


def _dot_tn_hi(a, b):
    return lax.dot_general(a, b, (((0,), (0,)), ((), ())), preferred_element_type=F32, precision=HIGHEST)


def _group_ones(width):
    i = lax.broadcasted_iota(jnp.int32, (LANES, LANES), 0) // width
    j = lax.broadcasted_iota(jnp.int32, (LANES, LANES), 1) // width
    return jnp.where(i == j, 1.0, 0.0).astype(BF16)


def _seg_sum(x, ones_bd):
    hi = x.astype(BF16)
    lo = (x - hi.astype(F32)).astype(BF16)
    cols = []
    for c in range(x.shape[-1] // LANES):
        sl = slice(c * LANES, (c + 1) * LANES)
        cols.append(_dot(hi[:, sl], ones_bd) + _dot(lo[:, sl], ones_bd))
    return cols[0] if len(cols) == 1 else jnp.concatenate(cols, axis=-1)


def _shift_rows(x, prev_rows, n):
    rolled = pltpu.roll(x, n, axis=0)
    row = lax.broadcasted_iota(jnp.int32, x.shape, 0)
    p = prev_rows.shape[0]
    for j in range(n):
        rolled = jnp.where(row == j, prev_rows[p - n + j:p - n + j + 1, :], rolled)
    return rolled


def _norm_matmul_kernel(x_ref, g_ref, w_ref, o_ref, h_ref):
    @pl.when(pl.program_id(1) == 0)
    def _():
        h_ref[...] = _rms(x_ref[...], g_ref[...]).astype(BF16)

    o_ref[...] = _dot(h_ref[...], w_ref[...])


def _norm_matmul(x, gain, w, name):
    m, d = x.shape
    n = w.shape[1]
    tm = min(1024, m)
    tn = 512
    return pl.pallas_call(
        _norm_matmul_kernel,
        grid=(m // tm, n // tn),
        in_specs=[
            pl.BlockSpec((tm, d), lambda i, j: (i, 0)),
            pl.BlockSpec((1, d), lambda i, j: (0, 0)),
            pl.BlockSpec((d, tn), lambda i, j: (0, j)),
        ],
        out_specs=pl.BlockSpec((tm, tn), lambda i, j: (i, j)),
        out_shape=jax.ShapeDtypeStruct((m, n), F32),
        scratch_shapes=[pltpu.VMEM((tm, d), BF16)],
        compiler_params=_params("parallel", "arbitrary"),
        name=name,
    )(x, gain, w)


def _gelu_tanh(x):
    return 0.5 * x * (1.0 + jnp.tanh(0.7978845608028654 * (x + 0.044715 * (x * x * x))))


def _gmlp_kernel(p_ref, vg_ref, ws_ref, bst_ref, og_ref, o_ref):
    groups, chunk, _ = ws_ref.shape
    tm = p_ref.shape[0]
    z = _gelu_tanh(p_ref[...])
    half = z.shape[1] // 2
    row = lax.broadcasted_iota(jnp.int32, (chunk, chunk), 0)
    col = lax.broadcasted_iota(jnp.int32, (chunk, chunk), 1)
    bst = bst_ref[...]
    for g in range(groups):
        sl = slice(g * LANES, (g + 1) * LANES)
        u = z[:, sl]
        v = _rms(z[:, half + g * LANES: half + (g + 1) * LANES], vg_ref[:, sl]).astype(BF16)
        w = jnp.where(col <= row, ws_ref[g], 0.0).astype(BF16)
        bias = bst[:, g:g + 1]
        for c in range(tm // chunk):
            rows = slice(c * chunk, (c + 1) * chunk)
            mixed = _dot(w, v[rows]) + bias
            o_ref[rows, sl] = _rms(u[rows] * mixed, og_ref[:, sl])


def _gmlp(pa, v_gain, ws, bs_t, out_gain):
    m, n2 = pa.shape
    groups, chunk, _ = ws.shape
    tm = min(256, m)
    return pl.pallas_call(
        _gmlp_kernel,
        grid=(m // tm,),
        in_specs=[
            pl.BlockSpec((tm, n2), lambda i: (i, 0)),
            pl.BlockSpec((1, n2 // 2), lambda i: (0, 0)),
            pl.BlockSpec((groups, chunk, chunk), lambda i: (0, 0, 0)),
            pl.BlockSpec((chunk, groups), lambda i: (0, 0)),
            pl.BlockSpec((1, n2 // 2), lambda i: (0, 0)),
        ],
        out_specs=pl.BlockSpec((tm, n2 // 2), lambda i: (i, 0)),
        out_shape=jax.ShapeDtypeStruct((m, n2 // 2), F32),
        compiler_params=_params("parallel"),
        name="gmlp_mix",
    )(pa, v_gain, ws, bs_t, out_gain)


def _conv_kernel(seq_len, p_ref, prev_ref, cw_ref, og_ref, o_ref):
    tm = p_ref.shape[0]
    c = o_ref.shape[1]
    p = p_ref[...]
    gb, z = p[:, :c], p[:, c:2 * c] * p[:, 2 * c:]
    pv = prev_ref[...]
    seq_start = (pl.program_id(0) * tm) % seq_len == 0
    zp = jnp.where(seq_start, 0.0, pv[:, c:2 * c] * pv[:, 2 * c:])
    cw = cw_ref[...]
    taps = cw.shape[0]
    zc = cw[taps - 1:taps, :] * z
    for s in range(1, taps):
        zc = zc + cw[taps - 1 - s:taps - s, :] * _shift_rows(z, zp, s)
    y = gb * zc
    ms = _seg_sum(y * y, _group_ones(CONV_GROUP_W)) * (1.0 / CONV_GROUP_W)
    o_ref[...] = y * lax.rsqrt(ms + RMS_EPS) * og_ref[...]


def _short_conv(pc, conv_w, out_gain, seq_len):
    m, n3 = pc.shape
    c = n3 // 3
    tm = min(256, m)
    rb = tm // SUBLANES
    return pl.pallas_call(
        functools.partial(_conv_kernel, seq_len),
        grid=(m // tm,),
        in_specs=[
            pl.BlockSpec((tm, n3), lambda i: (i, 0)),
            pl.BlockSpec((SUBLANES, n3), lambda i: (jnp.maximum(i * rb - 1, 0), 0)),
            pl.BlockSpec(conv_w.shape, lambda i: (0, 0)),
            pl.BlockSpec((1, c), lambda i: (0, 0)),
        ],
        out_specs=pl.BlockSpec((tm, c), lambda i: (i, 0)),
        out_shape=jax.ShapeDtypeStruct((m, c), F32),
        compiler_params=_params("parallel"),
        name="short_conv_mix",
    )(pc, pc, conv_w, out_gain)


def _rwkv_prep_kernel(seq_len, n_decay, p_ref, prev_ref, mu_ref, wwa_ref, gup_ref, w0_ref, a0_ref, kk_ref,
                      ka_ref, r_o, k_o, v_o, lw_o, a_o, b_o, g_o):
    tm = p_ref.shape[0]
    c = r_o.shape[1]
    p = p_ref[...]
    seq_start = (pl.program_id(0) * tm) % seq_len == 0
    pv = jnp.where(seq_start, 0.0, prev_ref[...])
    ps = p + (_shift_rows(p, pv, 1) - p) * mu_ref[...]
    r, k, v = ps[:, :c], ps[:, c:2 * c], ps[:, 2 * c:3 * c]
    lo = ps[:, 3 * c:3 * c + wwa_ref.shape[0]]
    gd = ps[:, 3 * c + wwa_ref.shape[0]:]
    lane = lax.broadcasted_iota(jnp.int32, lo.shape, 1)
    act = jnp.where(lane < n_decay, jnp.tanh(lo), lo)
    wa = _dot(act.astype(BF16), wwa_ref[...])
    zw = w0_ref[...] + wa[:, :c]
    w = -(jnp.maximum(-zw, 0.0) + jnp.log(1.0 + jnp.exp(-jnp.abs(zw)))) - 0.5
    a = 1.0 / (1.0 + jnp.exp(-(a0_ref[...] + wa[:, c:])))
    g = _dot((1.0 / (1.0 + jnp.exp(-gd))).astype(BF16), gup_ref[...])
    kkf = k * kk_ref[...]
    nrm = jnp.sqrt(_seg_sum(kkf * kkf, _group_ones(RWKV_HEAD)))
    kk = kkf / jnp.maximum(nrm, 1e-12)
    r_o[...] = r
    k_o[...] = k * (1.0 + (a - 1.0) * ka_ref[...])
    v_o[...] = v
    lw_o[...] = -jnp.exp(w)
    a_o[...] = -kk
    b_o[...] = kk * a
    g_o[...] = g


def _rwkv_prep(pb, mu, wwa, g_up, w0, a0, k_k, k_a, seq_len, n_decay):
    m, nb = pb.shape
    c = w0.shape[1]
    tm = min(256, m)
    rb = tm // SUBLANES
    vec = lambda n: pl.BlockSpec((1, n), lambda i: (0, 0))
    full = lambda a: pl.BlockSpec(a.shape, lambda i: (0, 0))
    out = jax.ShapeDtypeStruct((m, c), F32)
    return pl.pallas_call(
        functools.partial(_rwkv_prep_kernel, seq_len, n_decay),
        grid=(m // tm,),
        in_specs=[
            pl.BlockSpec((tm, nb), lambda i: (i, 0)),
            pl.BlockSpec((SUBLANES, nb), lambda i: (jnp.maximum(i * rb - 1, 0), 0)),
            vec(nb), full(wwa), full(g_up), vec(c), vec(c), vec(c), vec(c),
        ],
        out_specs=[pl.BlockSpec((tm, c), lambda i: (i, 0))] * 7,
        out_shape=[out] * 7,
        compiler_params=_params("parallel"),
        name="rwkv_prep",
    )(pb, pb, mu, wwa, g_up, w0, a0, k_k, k_a)


def _wkv_kernel(r_ref, k_ref, v_ref, lw_ref, a_ref, b_ref, g_ref, lnw_ref, lnb_ref, rk_ref, o_ref, h_ref):
    @pl.when(pl.program_id(1) == 0)
    def _():
        h_ref[...] = jnp.zeros_like(h_ref)

    L, C = r_ref.shape
    r, k, v, lw, a, b = r_ref[...], k_ref[...], v_ref[...], lw_ref[...], a_ref[...], b_ref[...]
    row = lax.broadcasted_iota(jnp.int32, (L, L), 0)
    col = lax.broadcasted_iota(jnp.int32, (L, L), 1)
    lower = col <= row
    strict = col < row
    eye = jnp.where(col == row, 1.0, 0.0)
    cum = _dot_hi(jnp.where(lower, 1.0, 0.0), lw)
    c_mid = cum[L // 2 - 1:L // 2, :]
    c_end = cum[L - 1:L, :]
    e_fwd = jnp.exp(cum - c_mid)
    e_bwd = jnp.exp(c_mid - cum)
    e_end = jnp.exp(c_end - cum)
    rt = r * e_fwd
    at = a * jnp.exp(cum - lw - c_mid)
    bt = b * e_bwd
    kt = k * e_bwd
    bh = b * e_end
    kh = k * e_end
    dec = jnp.exp(c_end)
    dec_mid = jnp.exp(c_mid)

    lane = lax.broadcasted_iota(jnp.int32, (1, LANES), 1)
    hi = lax.broadcasted_iota(jnp.int32, (LANES, LANES), 0) // RWKV_HEAD
    hj = lax.broadcasted_iota(jnp.int32, (LANES, LANES), 1) // RWKV_HEAD
    same_head = hi == hj
    zeros_t = jnp.zeros((L, LANES), F32)
    n_double = (L - 1).bit_length() - 1

    ys = []
    for t in range(C // LANES):
        sl = slice(t * LANES, (t + 1) * LANES)
        rt_t, at_t, bt_t, kt_t, v_t = rt[:, sl], at[:, sl], bt[:, sl], kt[:, sl], v[:, sl]
        rhs_t = jnp.concatenate([bt_t, kt_t], axis=0)
        a_acc, u_acc, r_acc, y_acc = zeros_t, zeros_t, rt_t, zeros_t
        for hh in range(LANES // RWKV_HEAD):
            m = (lane // RWKV_HEAD) == hh
            am = jnp.where(m, at_t, 0.0)
            vm = jnp.where(m, v_t, 0.0)
            s = _dot_nt_hi(jnp.concatenate([am, jnp.where(m, rt_t, 0.0)], axis=0), rhs_t)
            n_ab = jnp.where(strict, s[:L, :L], 0.0)
            a_ak = jnp.where(strict, s[:L, L:], 0.0)
            m_rb = jnp.where(lower, s[L:, :L], 0.0)
            m_rk = jnp.where(lower, s[L:, L:], 0.0)
            tinv = eye + n_ab
            pw = n_ab
            for _ in range(n_double):
                pw = _dot_hi(pw, pw)
                tinv = tinv + _dot_hi(tinv, pw)
            x = _dot_hi(tinv, jnp.concatenate([am, _dot_hi(a_ak, vm)], axis=1))
            zz = _dot_hi(m_rb, x)
            a_acc = a_acc + x[:, :LANES]
            u_acc = u_acc + x[:, LANES:]
            r_acc = r_acc + zz[:, :LANES]
            y_acc = y_acc + zz[:, LANES:] + _dot_hi(m_rk, vm)
        h = h_ref[t]
        h_mid = h * jnp.transpose(jnp.broadcast_to(dec_mid[:, sl], (LANES, LANES)))
        ah = _dot_hi(jnp.concatenate([a_acc, r_acc], axis=0), h_mid)
        u = ah[:L] + u_acc
        ys.append(ah[L:] + y_acc)
        upd = _dot_tn_hi(jnp.concatenate([bh[:, sl], kh[:, sl]], axis=0), jnp.concatenate([u, v_t], axis=0))
        dec_rows = jnp.transpose(jnp.broadcast_to(dec[:, sl], (LANES, LANES)))
        h_ref[t] = h * dec_rows + jnp.where(same_head, upd, 0.0)

    y = jnp.concatenate(ys, axis=1)
    ones_bd = _group_ones(RWKV_HEAD)
    inv_n = 1.0 / RWKV_HEAD
    mean = _seg_sum(y, ones_bd) * inv_n
    d = y - mean
    var = _seg_sum(d * d, ones_bd) * inv_n
    yn = d * lax.rsqrt(var + LNX_EPS) * lnw_ref[...] + lnb_ref[...]
    bonus = _seg_sum(r * k * rk_ref[...], ones_bd) * v
    o_ref[...] = (yn + bonus) * g_ref[...]


def _wkv(r, k, v, lw, a, b, g, ln_w, ln_b, r_k, batch, seq_len):
    m, c = r.shape
    L = min(WKV_CHUNK, seq_len)
    nc = seq_len // L
    blk = pl.BlockSpec((L, c), lambda bi, ci: (bi * nc + ci, 0))
    vec = pl.BlockSpec((1, c), lambda bi, ci: (0, 0))
    return pl.pallas_call(
        _wkv_kernel,
        grid=(batch, nc),
        in_specs=[blk] * 7 + [vec] * 3,
        out_specs=blk,
        out_shape=jax.ShapeDtypeStruct((m, c), F32),
        scratch_shapes=[pltpu.VMEM((c // LANES, LANES, LANES), F32)],
        compiler_params=_params("parallel", "arbitrary"),
        name="wkv7_scan",
    )(r, k, v, lw, a, b, g, ln_w, ln_b, r_k)


def _out_proj_kernel(ya_ref, yb_ref, yc_ref, wa_ref, wb_ref, wc_ref, g_ref, x_ref, o_ref):
    y = (_dot(ya_ref[...].astype(BF16), wa_ref[...]) + _dot(yb_ref[...].astype(BF16), wb_ref[...])
         + _dot(yc_ref[...].astype(BF16), wc_ref[...]))
    o_ref[...] = x_ref[...] + _rms(y, g_ref[...])


def _out_proj(ya, yb, yc, wa, wb, wc, gain, x):
    m, d = x.shape
    tm = min(256, m)
    act = lambda a: pl.BlockSpec((tm, a.shape[1]), lambda i: (i, 0))
    full = lambda a: pl.BlockSpec(a.shape, lambda i: (0, 0))
    return pl.pallas_call(
        _out_proj_kernel,
        grid=(m // tm,),
        in_specs=[act(ya), act(yb), act(yc), full(wa), full(wb), full(wc), full(gain), act(x)],
        out_specs=act(x),
        out_shape=jax.ShapeDtypeStruct((m, d), F32),
        compiler_params=_params("parallel"),
        name="out_proj",
    )(ya, yb, yc, wa, wb, wc, gain, x)


def _mlp_kernel(x_ref, gpre_ref, wu_ref, wd_ref, gpost_ref, o_ref, h_ref, acc_ref):
    j = pl.program_id(1)

    @pl.when(j == 0)
    def _():
        h_ref[...] = _rms(x_ref[...], gpre_ref[...]).astype(BF16)
        acc_ref[...] = jnp.zeros_like(acc_ref)

    f = jnp.maximum(_dot(h_ref[...], wu_ref[...]), 0.0)
    acc_ref[...] += _dot((f * f).astype(BF16), wd_ref[...])

    @pl.when(j == pl.num_programs(1) - 1)
    def _():
        o_ref[...] = x_ref[...] + _rms(acc_ref[...], gpost_ref[...])


def _mlp(x, g_pre, w_up, w_down, g_post):
    m, d = x.shape
    ff = w_up.shape[1]
    tm = min(512, m)
    tf = min(512, ff)
    return pl.pallas_call(
        _mlp_kernel,
        grid=(m // tm, ff // tf),
        in_specs=[
            pl.BlockSpec((tm, d), lambda i, j: (i, 0)),
            pl.BlockSpec((1, d), lambda i, j: (0, 0)),
            pl.BlockSpec((d, tf), lambda i, j: (0, j)),
            pl.BlockSpec((tf, d), lambda i, j: (j, 0)),
            pl.BlockSpec((1, d), lambda i, j: (0, 0)),
        ],
        out_specs=pl.BlockSpec((tm, d), lambda i, j: (i, 0)),
        out_shape=jax.ShapeDtypeStruct((m, d), F32),
        scratch_shapes=[pltpu.VMEM((tm, d), BF16), pltpu.VMEM((tm, d), F32)],
        compiler_params=_params("parallel", "arbitrary"),
        name="relu2_mlp",
    )(x, g_pre, w_up, w_down, g_post)


def _pad_cols(w, n):
    return jnp.pad(w, ((0, 0), (0, n - w.shape[1])))


def kernel(x, norm_mix_pre, norm_mix_post, norm_mlp_pre, norm_mlp_post, w_in, gm_v_gain, gm_ws, gm_bs,
           gm_out_gain, rk_mu, rk_w0, rk_w_up, rk_a0, rk_a_up, rk_g_up, rk_k_k, rk_k_a, rk_r_k, rk_ln_w,
           rk_ln_b, sc_conv, sc_out_gain, w_out, mlp_up, mlp_down):
    batch, seq_len, d = x.shape
    depth = w_in.shape[0]
    mix_a = gm_v_gain.shape[1]
    mix_b = rk_w0.shape[1]
    mix_c = sc_out_gain.shape[1]
    n_decay, n_iclr, n_gate = rk_w_up.shape[1], rk_a_up.shape[1], rk_g_up.shape[1]
    n_lora = n_decay + n_iclr
    lora_pad = -(-n_lora // LANES) * LANES
    col_b = 2 * mix_a
    col_lora = col_b + 3 * mix_b
    col_gate = col_lora + n_lora
    col_c = col_gate + n_gate
    row = lambda a: a.reshape(1, -1)

    xf = x.reshape(batch * seq_len, d)
    for l in range(depth):
        wi = w_in[l]
        w_a = wi[:, :col_b].astype(BF16)
        w_b = jnp.concatenate([wi[:, col_b:col_lora], _pad_cols(wi[:, col_lora:col_gate], lora_pad),
                               wi[:, col_gate:col_c]], axis=1).astype(BF16)
        w_c = wi[:, col_c:].astype(BF16)
        mu = rk_mu[l]
        mu_b = jnp.concatenate([mu[:3 * mix_b], jnp.pad(mu[3 * mix_b:3 * mix_b + n_lora], (0, lora_pad - n_lora)),
                                mu[3 * mix_b + n_lora:]]).reshape(1, -1)
        wwa = jnp.zeros((lora_pad, 2 * mix_b), F32)
        wwa = wwa.at[:n_decay, :mix_b].set(rk_w_up[l]).at[n_decay:n_lora, mix_b:].set(rk_a_up[l]).astype(BF16)

        g_pre = row(norm_mix_pre[l])
        pa = _norm_matmul(xf, g_pre, w_a, "in_proj_gmlp")
        pb = _norm_matmul(xf, g_pre, w_b, "in_proj_rwkv")
        pc = _norm_matmul(xf, g_pre, w_c, "in_proj_conv")

        ya = _gmlp(pa, row(gm_v_gain[l]), gm_ws[l], gm_bs[l].T, row(gm_out_gain[l]))
        scan_in = _rwkv_prep(pb, mu_b, wwa, rk_g_up[l].astype(BF16), row(rk_w0[l]), row(rk_a0[l]),
                             row(rk_k_k[l]), row(rk_k_a[l]), seq_len, n_decay)
        yb = _wkv(*scan_in, row(rk_ln_w[l]), row(rk_ln_b[l]), row(rk_r_k[l]), batch, seq_len)
        yc = _short_conv(pc, sc_conv[l], row(sc_out_gain[l]), seq_len)

        wo = w_out[l].astype(BF16)
        xf = _out_proj(ya, yb, yc, wo[:mix_a], wo[mix_a:mix_a + mix_b], wo[mix_a + mix_b:],
                       row(norm_mix_post[l]), xf)
        xf = _mlp(xf, row(norm_mlp_pre[l]), mlp_up[l].astype(BF16), mlp_down[l].astype(BF16),
                  row(norm_mlp_post[l]))
    return xf.reshape(batch, seq_len, d)
```

```python
import functools

import jax
import jax.numpy as jnp
from jax import lax
from jax.experimental import pallas as pl
from jax.experimental.pallas import tpu as pltpu

F32 = jnp.float32
BF16 = jnp.bfloat16
HIGHEST = lax.Precision.HIGHEST

RMS_EPS = 1e-6
LNX_EPS = 64e-5
RWKV_HEAD = 64
CONV_GROUP_W = 64
LANES = 128
SUBLANES = 8
WKV_CHUNK = 128
VMEM_LIMIT = 56 * 1024 * 1024


def _params(*sem):
    return pltpu.CompilerParams(dimension_semantics=sem, vmem_limit_bytes=VMEM_LIMIT)


def _rms(x, gain):
    return x * lax.rsqrt(jnp.mean(x * x, axis=-1, keepdims=True) + RMS_EPS) * gain


def _dot(a, b):
    return jnp.dot(a, b, preferred_element_type=F32)


def _dot_hi(a, b):
    return jnp.dot(a, b, preferred_element_type=F32, precision=HIGHEST)


def _dot_nt(a, b):
    return lax.dot_general(a, b, (((1,), (1,)), ((), ())), preferred_element_type=F32)


def _dot_tn(a, b):
    return lax.dot_general(a, b, (((0,), (0,)), ((), ())), preferred_element_type=F32)


def _group_ones(width):
    i = lax.broadcasted_iota(jnp.int32, (LANES, LANES), 0) // width
    j = lax.broadcasted_iota(jnp.int32, (LANES, LANES), 1) // width
    return jnp.where(i == j, 1.0, 0.0).astype(BF16)


def _seg_sum(x, ones_bd):
    hi = x.astype(BF16)
    lo = (x - hi.astype(F32)).astype(BF16)
    cols = []
    for c in range(x.shape[-1] // LANES):
        sl = slice(c * LANES, (c + 1) * LANES)
        cols.append(_dot(hi[:, sl], ones_bd) + _dot(lo[:, sl], ones_bd))
    return cols[0] if len(cols) == 1 else jnp.concatenate(cols, axis=-1)


def _shift_rows(x, prev_rows, n):
    rolled = pltpu.roll(x, n, axis=0)
    row = lax.broadcasted_iota(jnp.int32, x.shape, 0)
    p = prev_rows.shape[0]
    for j in range(n):
        rolled = jnp.where(row == j, prev_rows[p - n + j:p - n + j + 1, :], rolled)
    return rolled


def _norm_matmul_kernel(x_ref, g_ref, w_ref, o_ref, h_ref):
    @pl.when(pl.program_id(1) == 0)
    def _():
        h_ref[...] = _rms(x_ref[...], g_ref[...]).astype(BF16)

    o_ref[...] = _dot(h_ref[...], w_ref[...])


def _norm_matmul(x, gain, w, name):
    m, d = x.shape
    n = w.shape[1]
    tm = min(1024, m)
    tn = 512
    return pl.pallas_call(
        _norm_matmul_kernel,
        grid=(m // tm, n // tn),
        in_specs=[
            pl.BlockSpec((tm, d), lambda i, j: (i, 0)),
            pl.BlockSpec((1, d), lambda i, j: (0, 0)),
            pl.BlockSpec((d, tn), lambda i, j: (0, j)),
        ],
        out_specs=pl.BlockSpec((tm, tn), lambda i, j: (i, j)),
        out_shape=jax.ShapeDtypeStruct((m, n), F32),
        scratch_shapes=[pltpu.VMEM((tm, d), BF16)],
        compiler_params=_params("parallel", "arbitrary"),
        name=name,
    )(x, gain, w)


def _gelu_tanh(x):
    return 0.5 * x * (1.0 + jnp.tanh(0.7978845608028654 * (x + 0.044715 * (x * x * x))))


def _gmlp_kernel(p_ref, vg_ref, ws_ref, bst_ref, og_ref, o_ref):
    groups, chunk, _ = ws_ref.shape
    tm = p_ref.shape[0]
    z = _gelu_tanh(p_ref[...])
    half = z.shape[1] // 2
    row = lax.broadcasted_iota(jnp.int32, (chunk, chunk), 0)
    col = lax.broadcasted_iota(jnp.int32, (chunk, chunk), 1)
    bst = bst_ref[...]
    for g in range(groups):
        sl = slice(g * LANES, (g + 1) * LANES)
        u = z[:, sl]
        v = _rms(z[:, half + g * LANES: half + (g + 1) * LANES], vg_ref[:, sl]).astype(BF16)
        w = jnp.where(col <= row, ws_ref[g], 0.0).astype(BF16)
        bias = bst[:, g:g + 1]
        for c in range(tm // chunk):
            rows = slice(c * chunk, (c + 1) * chunk)
            mixed = _dot(w, v[rows]) + bias
            o_ref[rows, sl] = _rms(u[rows] * mixed, og_ref[:, sl])


def _gmlp(pa, v_gain, ws, bs_t, out_gain):
    m, n2 = pa.shape
    groups, chunk, _ = ws.shape
    tm = min(256, m)
    return pl.pallas_call(
        _gmlp_kernel,
        grid=(m // tm,),
        in_specs=[
            pl.BlockSpec((tm, n2), lambda i: (i, 0)),
            pl.BlockSpec((1, n2 // 2), lambda i: (0, 0)),
            pl.BlockSpec((groups, chunk, chunk), lambda i: (0, 0, 0)),
            pl.BlockSpec((chunk, groups), lambda i: (0, 0)),
            pl.BlockSpec((1, n2 // 2), lambda i: (0, 0)),
        ],
        out_specs=pl.BlockSpec((tm, n2 // 2), lambda i: (i, 0)),
        out_shape=jax.ShapeDtypeStruct((m, n2 // 2), F32),
        compiler_params=_params("parallel"),
        name="gmlp_mix",
    )(pa, v_gain, ws, bs_t, out_gain)


def _conv_kernel(seq_len, p_ref, prev_ref, cw_ref, og_ref, o_ref):
    tm = p_ref.shape[0]
    c = o_ref.shape[1]
    p = p_ref[...]
    gb, z = p[:, :c], p[:, c:2 * c] * p[:, 2 * c:]
    pv = prev_ref[...]
    seq_start = (pl.program_id(0) * tm) % seq_len == 0
    zp = jnp.where(seq_start, 0.0, pv[:, c:2 * c] * pv[:, 2 * c:])
    cw = cw_ref[...]
    taps = cw.shape[0]
    zc = cw[taps - 1:taps, :] * z
    for s in range(1, taps):
        zc = zc + cw[taps - 1 - s:taps - s, :] * _shift_rows(z, zp, s)
    y = gb * zc
    ms = _seg_sum(y * y, _group_ones(CONV_GROUP_W)) * (1.0 / CONV_GROUP_W)
    o_ref[...] = y * lax.rsqrt(ms + RMS_EPS) * og_ref[...]


def _short_conv(pc, conv_w, out_gain, seq_len):
    m, n3 = pc.shape
    c = n3 // 3
    tm = min(256, m)
    rb = tm // SUBLANES
    return pl.pallas_call(
        functools.partial(_conv_kernel, seq_len),
        grid=(m // tm,),
        in_specs=[
            pl.BlockSpec((tm, n3), lambda i: (i, 0)),
            pl.BlockSpec((SUBLANES, n3), lambda i: (jnp.maximum(i * rb - 1, 0), 0)),
            pl.BlockSpec(conv_w.shape, lambda i: (0, 0)),
            pl.BlockSpec((1, c), lambda i: (0, 0)),
        ],
        out_specs=pl.BlockSpec((tm, c), lambda i: (i, 0)),
        out_shape=jax.ShapeDtypeStruct((m, c), F32),
        compiler_params=_params("parallel"),
        name="short_conv_mix",
    )(pc, pc, conv_w, out_gain)


def _rwkv_prep_kernel(seq_len, n_decay, p_ref, prev_ref, mu_ref, wwa_ref, gup_ref, w0_ref, a0_ref, kk_ref,
                      ka_ref, r_o, k_o, v_o, lw_o, a_o, b_o, g_o):
    tm = p_ref.shape[0]
    c = r_o.shape[1]
    p = p_ref[...]
    seq_start = (pl.program_id(0) * tm) % seq_len == 0
    pv = jnp.where(seq_start, 0.0, prev_ref[...])
    ps = p + (_shift_rows(p, pv, 1) - p) * mu_ref[...]
    r, k, v = ps[:, :c], ps[:, c:2 * c], ps[:, 2 * c:3 * c]
    lo = ps[:, 3 * c:3 * c + wwa_ref.shape[0]]
    gd = ps[:, 3 * c + wwa_ref.shape[0]:]
    lane = lax.broadcasted_iota(jnp.int32, lo.shape, 1)
    act = jnp.where(lane < n_decay, jnp.tanh(lo), lo)
    wa = _dot(act.astype(BF16), wwa_ref[...])
    zw = w0_ref[...] + wa[:, :c]
    w = -(jnp.maximum(-zw, 0.0) + jnp.log(1.0 + jnp.exp(-jnp.abs(zw)))) - 0.5
    a = 1.0 / (1.0 + jnp.exp(-(a0_ref[...] + wa[:, c:])))
    g = _dot((1.0 / (1.0 + jnp.exp(-gd))).astype(BF16), gup_ref[...])
    kkf = k * kk_ref[...]
    nrm = jnp.sqrt(_seg_sum(kkf * kkf, _group_ones(RWKV_HEAD)))
    kk = kkf / jnp.maximum(nrm, 1e-12)
    r_o[...] = r
    k_o[...] = k * (1.0 + (a - 1.0) * ka_ref[...])
    v_o[...] = v
    lw_o[...] = -jnp.exp(w)
    a_o[...] = -kk
    b_o[...] = kk * a
    g_o[...] = g


def _rwkv_prep(pb, mu, wwa, g_up, w0, a0, k_k, k_a, seq_len, n_decay):
    m, nb = pb.shape
    c = w0.shape[1]
    tm = min(256, m)
    rb = tm // SUBLANES
    vec = lambda n: pl.BlockSpec((1, n), lambda i: (0, 0))
    full = lambda a: pl.BlockSpec(a.shape, lambda i: (0, 0))
    out = jax.ShapeDtypeStruct((m, c), F32)
    return pl.pallas_call(
        functools.partial(_rwkv_prep_kernel, seq_len, n_decay),
        grid=(m // tm,),
        in_specs=[
            pl.BlockSpec((tm, nb), lambda i: (i, 0)),
            pl.BlockSpec((SUBLANES, nb), lambda i: (jnp.maximum(i * rb - 1, 0), 0)),
            vec(nb), full(wwa), full(g_up), vec(c), vec(c), vec(c), vec(c),
        ],
        out_specs=[pl.BlockSpec((tm, c), lambda i: (i, 0))] * 7,
        out_shape=[out] * 7,
        compiler_params=_params("parallel"),
        name="rwkv_prep",
    )(pb, pb, mu, wwa, g_up, w0, a0, k_k, k_a)


def _wkv_kernel(r_ref, k_ref, v_ref, lw_ref, a_ref, b_ref, g_ref, lnw_ref, lnb_ref, rk_ref, o_ref, h_ref):
    @pl.when(pl.program_id(1) == 0)
    def _():
        h_ref[...] = jnp.zeros_like(h_ref)

    L, C = r_ref.shape
    r, k, v, lw, a, b = r_ref[...], k_ref[...], v_ref[...], lw_ref[...], a_ref[...], b_ref[...]
    row = lax.broadcasted_iota(jnp.int32, (L, L), 0)
    col = lax.broadcasted_iota(jnp.int32, (L, L), 1)
    lower = col <= row
    strict = col < row
    eye = jnp.where(col == row, 1.0, 0.0)
    cum = _dot_hi(jnp.where(lower, 1.0, 0.0), lw)
    c_mid = cum[L // 2 - 1:L // 2, :]
    c_end = cum[L - 1:L, :]
    e_fwd = jnp.exp(cum - c_mid)
    e_bwd = jnp.exp(c_mid - cum)
    e_end = jnp.exp(c_end - cum)
    bf = lambda z: z.astype(BF16)
    rt = bf(r * e_fwd)
    at = bf(a * jnp.exp(cum - lw - c_mid))
    bt = bf(b * e_bwd)
    kt = bf(k * e_bwd)
    bh = bf(b * e_end)
    kh = bf(k * e_end)
    vb = bf(v)
    dec = jnp.exp(c_end)
    dec_mid = jnp.exp(c_mid)

    lane = lax.broadcasted_iota(jnp.int32, (1, LANES), 1)
    hi = lax.broadcasted_iota(jnp.int32, (LANES, LANES), 0) // RWKV_HEAD
    hj = lax.broadcasted_iota(jnp.int32, (LANES, LANES), 1) // RWKV_HEAD
    same_head = hi == hj
    zeros_t = jnp.zeros((L, LANES), F32)
    zero_b = jnp.zeros((), BF16)
    same_blk = lambda size: (row // size) == (col // size)

    tiles = range(C // LANES)
    per_tile = LANES // RWKV_HEAD
    heads = [(t, hh) for t in tiles for hh in range(per_tile)]
    tile_sl = [slice(t * LANES, (t + 1) * LANES) for t in tiles]
    head_mask = [(lane // RWKV_HEAD) == hh for hh in range(per_tile)]
    am = [jnp.where(head_mask[hh], at[:, tile_sl[t]], zero_b) for t, hh in heads]
    vm = [jnp.where(head_mask[hh], vb[:, tile_sl[t]], zero_b) for t, hh in heads]
    rhs = [jnp.concatenate([bt[:, tile_sl[t]], kt[:, tile_sl[t]]], axis=0) for t in tiles]
    s = [_dot_nt(jnp.concatenate([am[i], jnp.where(head_mask[hh], rt[:, tile_sl[t]], zero_b)], axis=0), rhs[t])
         for i, (t, hh) in enumerate(heads)]
    n_ab = [jnp.where(strict, si[:L, :L], 0.0) for si in s]
    a_ak = [bf(jnp.where(strict, si[:L, L:], 0.0)) for si in s]
    m_rb = [bf(jnp.where(lower, si[L:, :L], 0.0)) for si in s]
    m_rk = [bf(jnp.where(lower, si[L:, L:], 0.0)) for si in s]
    tinv = [eye + jnp.where(same_blk(2), n, 0.0) for n in n_ab]
    size = 2
    while size < L:
        sub_diag = same_blk(2 * size) & jnp.logical_not(same_blk(size))
        e = [bf(jnp.where(sub_diag, n, 0.0)) for n in n_ab]
        tb = [bf(ti) for ti in tinv]
        te = [bf(_dot(tb_, e_)) for tb_, e_ in zip(tb, e)]
        tinv = [ti + _dot(te_, tb_) for ti, te_, tb_ in zip(tinv, te, tb)]
        size *= 2
    w = [bf(_dot(ak, v_)) for ak, v_ in zip(a_ak, vm)]
    x = [_dot(bf(ti), jnp.concatenate([a_, w_], axis=1)) for ti, a_, w_ in zip(tinv, am, w)]
    zz = [_dot(mb, bf(x_)) for mb, x_ in zip(m_rb, x)]
    yk = [_dot(mk, v_) for mk, v_ in zip(m_rk, vm)]

    def tile_sum(vals, t):
        out = vals[t * per_tile]
        for j in range(1, per_tile):
            out = out + vals[t * per_tile + j]
        return out

    a_acc = [tile_sum([x_[:, :LANES] for x_ in x], t) for t in tiles]
    u_acc = [tile_sum([x_[:, LANES:] for x_ in x], t) for t in tiles]
    r_acc = [rt[:, tile_sl[t]].astype(F32) + tile_sum([z_[:, :LANES] for z_ in zz], t) for t in tiles]
    y_acc = [tile_sum([z_[:, LANES:] + y_ for z_, y_ in zip(zz, yk)], t) for t in tiles]
    h = [h_ref[t] for t in tiles]
    h_mid = [bf(h[t] * jnp.transpose(jnp.broadcast_to(dec_mid[:, tile_sl[t]], (LANES, LANES)))) for t in tiles]
    ah = [_dot(bf(jnp.concatenate([a_acc[t], r_acc[t]], axis=0)), h_mid[t]) for t in tiles]
    u = [ah[t][:L] + u_acc[t] for t in tiles]
    upd = [_dot_tn(jnp.concatenate([bh[:, tile_sl[t]], kh[:, tile_sl[t]]], axis=0),
                   jnp.concatenate([bf(u[t]), vb[:, tile_sl[t]]], axis=0)) for t in tiles]
    for t in tiles:
        dec_rows = jnp.transpose(jnp.broadcast_to(dec[:, tile_sl[t]], (LANES, LANES)))
        h_ref[t] = h[t] * dec_rows + jnp.where(same_head, upd[t], 0.0)

    y = jnp.concatenate([ah[t][L:] + y_acc[t] for t in tiles], axis=1)
    ones_bd = _group_ones(RWKV_HEAD)
    inv_n = 1.0 / RWKV_HEAD
    mean = _seg_sum(y, ones_bd) * inv_n
    d = y - mean
    var = _seg_sum(d * d, ones_bd) * inv_n
    yn = d * lax.rsqrt(var + LNX_EPS) * lnw_ref[...] + lnb_ref[...]
    bonus = _seg_sum(r * k * rk_ref[...], ones_bd) * v
    o_ref[...] = (yn + bonus) * g_ref[...]


def _wkv(r, k, v, lw, a, b, g, ln_w, ln_b, r_k, batch, seq_len):
    m, c = r.shape
    L = min(WKV_CHUNK, seq_len)
    assert L & (L - 1) == 0 and seq_len % L == 0, "chunk length must be a power of two dividing the sequence"
    nc = seq_len // L
    blk = pl.BlockSpec((L, c), lambda bi, ci: (bi * nc + ci, 0))
    vec = pl.BlockSpec((1, c), lambda bi, ci: (0, 0))
    return pl.pallas_call(
        _wkv_kernel,
        grid=(batch, nc),
        in_specs=[blk] * 7 + [vec] * 3,
        out_specs=blk,
        out_shape=jax.ShapeDtypeStruct((m, c), F32),
        scratch_shapes=[pltpu.VMEM((c // LANES, LANES, LANES), F32)],
        compiler_params=_params("parallel", "arbitrary"),
        name="wkv7_scan",
    )(r, k, v, lw, a, b, g, ln_w, ln_b, r_k)


def _out_proj_kernel(ya_ref, yb_ref, yc_ref, wa_ref, wb_ref, wc_ref, g_ref, x_ref, o_ref):
    y = (_dot(ya_ref[...].astype(BF16), wa_ref[...]) + _dot(yb_ref[...].astype(BF16), wb_ref[...])
         + _dot(yc_ref[...].astype(BF16), wc_ref[...]))
    o_ref[...] = x_ref[...] + _rms(y, g_ref[...])


def _out_proj(ya, yb, yc, wa, wb, wc, gain, x):
    m, d = x.shape
    tm = min(256, m)
    act = lambda a: pl.BlockSpec((tm, a.shape[1]), lambda i: (i, 0))
    full = lambda a: pl.BlockSpec(a.shape, lambda i: (0, 0))
    return pl.pallas_call(
        _out_proj_kernel,
        grid=(m // tm,),
        in_specs=[act(ya), act(yb), act(yc), full(wa), full(wb), full(wc), full(gain), act(x)],
        out_specs=act(x),
        out_shape=jax.ShapeDtypeStruct((m, d), F32),
        compiler_params=_params("parallel"),
        name="out_proj",
    )(ya, yb, yc, wa, wb, wc, gain, x)


def _mlp_kernel(x_ref, gpre_ref, wu_ref, wd_ref, gpost_ref, o_ref, h_ref, acc_ref):
    j = pl.program_id(1)

    @pl.when(j == 0)
    def _():
        h_ref[...] = _rms(x_ref[...], gpre_ref[...]).astype(BF16)
        acc_ref[...] = jnp.zeros_like(acc_ref)

    f = jnp.maximum(_dot(h_ref[...], wu_ref[...]), 0.0)
    acc_ref[...] += _dot((f * f).astype(BF16), wd_ref[...])

    @pl.when(j == pl.num_programs(1) - 1)
    def _():
        o_ref[...] = x_ref[...] + _rms(acc_ref[...], gpost_ref[...])


def _mlp(x, g_pre, w_up, w_down, g_post):
    m, d = x.shape
    ff = w_up.shape[1]
    tm = min(512, m)
    tf = min(512, ff)
    return pl.pallas_call(
        _mlp_kernel,
        grid=(m // tm, ff // tf),
        in_specs=[
            pl.BlockSpec((tm, d), lambda i, j: (i, 0)),
            pl.BlockSpec((1, d), lambda i, j: (0, 0)),
            pl.BlockSpec((d, tf), lambda i, j: (0, j)),
            pl.BlockSpec((tf, d), lambda i, j: (j, 0)),
            pl.BlockSpec((1, d), lambda i, j: (0, 0)),
        ],
        out_specs=pl.BlockSpec((tm, d), lambda i, j: (i, 0)),
        out_shape=jax.ShapeDtypeStruct((m, d), F32),
        scratch_shapes=[pltpu.VMEM((tm, d), BF16), pltpu.VMEM((tm, d), F32)],
        compiler_params=_params("parallel", "arbitrary"),
        name="relu2_mlp",
    )(x, g_pre, w_up, w_down, g_post)


def _pad_cols(w, n):
    return jnp.pad(w, ((0, 0), (0, n - w.shape[1])))


def kernel(x, norm_mix_pre, norm_mix_post, norm_mlp_pre, norm_mlp_post, w_in, gm_v_gain, gm_ws, gm_bs,
           gm_out_gain, rk_mu, rk_w0, rk_w_up, rk_a0, rk_a_up, rk_g_up, rk_k_k, rk_k_a, rk_r_k, rk_ln_w,
           rk_ln_b, sc_conv, sc_out_gain, w_out, mlp_up, mlp_down):
    batch, seq_len, d = x.shape
    depth = w_in.shape[0]
    mix_a = gm_v_gain.shape[1]
    mix_b = rk_w0.shape[1]
    mix_c = sc_out_gain.shape[1]
    n_decay, n_iclr, n_gate = rk_w_up.shape[1], rk_a_up.shape[1], rk_g_up.shape[1]
    n_lora = n_decay + n_iclr
    lora_pad = -(-n_lora // LANES) * LANES
    col_b = 2 * mix_a
    col_lora = col_b + 3 * mix_b
    col_gate = col_lora + n_lora
    col_c = col_gate + n_gate
    row = lambda a: a.reshape(1, -1)

    xf = x.reshape(batch * seq_len, d)
    for l in range(depth):
        wi = w_in[l]
        w_a = wi[:, :col_b].astype(BF16)
        w_b = jnp.concatenate([wi[:, col_b:col_lora], _pad_cols(wi[:, col_lora:col_gate], lora_pad),
                               wi[:, col_gate:col_c]], axis=1).astype(BF16)
        w_c = wi[:, col_c:].astype(BF16)
        mu = rk_mu[l]
        mu_b = jnp.concatenate([mu[:3 * mix_b], jnp.pad(mu[3 * mix_b:3 * mix_b + n_lora], (0, lora_pad - n_lora)),
                                mu[3 * mix_b + n_lora:]]).reshape(1, -1)
        wwa = jnp.zeros((lora_pad, 2 * mix_b), F32)
        wwa = wwa.at[:n_decay, :mix_b].set(rk_w_up[l]).at[n_decay:n_lora, mix_b:].set(rk_a_up[l]).astype(BF16)

        g_pre = row(norm_mix_pre[l])
        pa = _norm_matmul(xf, g_pre, w_a, "in_proj_gmlp")
        pb = _norm_matmul(xf, g_pre, w_b, "in_proj_rwkv")
        pc = _norm_matmul(xf, g_pre, w_c, "in_proj_conv")

        ya = _gmlp(pa, row(gm_v_gain[l]), gm_ws[l], gm_bs[l].T, row(gm_out_gain[l]))
        scan_in = _rwkv_prep(pb, mu_b, wwa, rk_g_up[l].astype(BF16), row(rk_w0[l]), row(rk_a0[l]),
                             row(rk_k_k[l]), row(rk_k_a[l]), seq_len, n_decay)
        yb = _wkv(*scan_in, row(rk_ln_w[l]), row(rk_ln_b[l]), row(rk_r_k[l]), batch, seq_len)
        yc = _short_conv(pc, sc_conv[l], row(sc_out_gain[l]), seq_len)

        wo = w_out[l].astype(BF16)
        xf = _out_proj(ya, yb, yc, wo[:mix_a], wo[mix_a:mix_a + mix_b], wo[mix_a + mix_b:],
                       row(norm_mix_post[l]), xf)
        xf = _mlp(xf, row(norm_mlp_pre[l]), mlp_up[l].astype(BF16), mlp_down[l].astype(BF16),
                  row(norm_mlp_post[l]))
    return xf.reshape(batch, seq_len, d)
```

```python
import functools

import jax
import jax.numpy as jnp
from jax import lax
from jax.experimental import pallas as pl
from jax.experimental.pallas import tpu as pltpu

F32 = jnp.float32
BF16 = jnp.bfloat16
HIGHEST = lax.Precision.HIGHEST

RMS_EPS = 1e-6
LNX_EPS = 64e-5
RWKV_HEAD = 64
CONV_GROUP_W = 64
LANES = 128
SUBLANES = 8
WKV_CHUNK = 128
VMEM_LIMIT = 56 * 1024 * 1024


def _params(*sem):
    return pltpu.CompilerParams(dimension_semantics=sem, vmem_limit_bytes=VMEM_LIMIT)


def _rms(x, gain):
    return x * lax.rsqrt(jnp.mean(x * x, axis=-1, keepdims=True) + RMS_EPS) * gain


def _dot(a, b):
    return jnp.dot(a, b, preferred_element_type=F32)


def _dot_hi(a, b):
    return jnp.dot(a, b, preferred_element_type=F32, precision=HIGHEST)


def _dot_nt(a, b):
    return lax.dot_general(a, b, (((1,), (1,)), ((), ())), preferred_element_type=F32)


def _dot_tn(a, b):
    return lax.dot_general(a, b, (((0,), (0,)), ((), ())), preferred_element_type=F32)


def _group_ones(width):
    i = lax.broadcasted_iota(jnp.int32, (LANES, LANES), 0) // width
    j = lax.broadcasted_iota(jnp.int32, (LANES, LANES), 1) // width
    return jnp.where(i == j, 1.0, 0.0).astype(BF16)


def _seg_sum(x, ones_bd):
    hi = x.astype(BF16)
    lo = (x - hi.astype(F32)).astype(BF16)
    cols = []
    for c in range(x.shape[-1] // LANES):
        sl = slice(c * LANES, (c + 1) * LANES)
        cols.append(_dot(hi[:, sl], ones_bd) + _dot(lo[:, sl], ones_bd))
    return cols[0] if len(cols) == 1 else jnp.concatenate(cols, axis=-1)


def _shift_rows(x, prev_rows, n):
    rolled = pltpu.roll(x, n, axis=0)
    row = lax.broadcasted_iota(jnp.int32, x.shape, 0)
    p = prev_rows.shape[0]
    for j in range(n):
        rolled = jnp.where(row == j, prev_rows[p - n + j:p - n + j + 1, :], rolled)
    return rolled


def _norm_matmul_kernel(x_ref, g_ref, w_ref, o_ref, h_ref):
    @pl.when(pl.program_id(1) == 0)
    def _():
        h_ref[...] = _rms(x_ref[...], g_ref[...]).astype(BF16)

    o_ref[...] = _dot(h_ref[...], w_ref[...])


def _norm_matmul(x, gain, w_all, layer):
    m, d = x.shape
    n = w_all.shape[2]
    tm = min(1024, m)
    tn = 1024
    return pl.pallas_call(
        _norm_matmul_kernel,
        grid=(m // tm, n // tn),
        in_specs=[
            pl.BlockSpec((tm, d), lambda i, j: (i, 0)),
            pl.BlockSpec((1, d), lambda i, j: (0, 0)),
            pl.BlockSpec((None, d, tn), lambda i, j: (layer, 0, j)),
        ],
        out_specs=pl.BlockSpec((tm, tn), lambda i, j: (i, j)),
        out_shape=jax.ShapeDtypeStruct((m, n), F32),
        scratch_shapes=[pltpu.VMEM((tm, d), BF16)],
        compiler_params=_params("parallel", "arbitrary"),
        name="in_proj",
    )(x, gain, w_all)


def _gelu_tanh(x):
    return 0.5 * x * (1.0 + jnp.tanh(0.7978845608028654 * (x + 0.044715 * (x * x * x))))


def _gmlp_kernel(pu_ref, pv_ref, vg_ref, ws_ref, bst_ref, og_ref, o_ref):
    groups, chunk, _ = ws_ref.shape
    tm = pu_ref.shape[0]
    zu = _gelu_tanh(pu_ref[...])
    zv = _gelu_tanh(pv_ref[...])
    row = lax.broadcasted_iota(jnp.int32, (chunk, chunk), 0)
    col = lax.broadcasted_iota(jnp.int32, (chunk, chunk), 1)
    bst = bst_ref[...]
    for g in range(groups):
        sl = slice(g * LANES, (g + 1) * LANES)
        u = zu[:, sl]
        v = _rms(zv[:, sl], vg_ref[:, sl]).astype(BF16)
        w = jnp.where(col <= row, ws_ref[g], 0.0).astype(BF16)
        bias = bst[:, g:g + 1]
        for c in range(tm // chunk):
            rows = slice(c * chunk, (c + 1) * chunk)
            mixed = _dot(w, v[rows]) + bias
            o_ref[rows, sl] = _rms(u[rows] * mixed, og_ref[:, sl])


def _gmlp(p, col0, v_gain, ws, bs_t, out_gain):
    m = p.shape[0]
    groups, chunk, _ = ws.shape
    half = v_gain.shape[1]
    cb = col0 // half
    tm = min(256, m)
    return pl.pallas_call(
        _gmlp_kernel,
        grid=(m // tm,),
        in_specs=[
            pl.BlockSpec((tm, half), lambda i: (i, cb)),
            pl.BlockSpec((tm, half), lambda i: (i, cb + 1)),
            pl.BlockSpec((1, half), lambda i: (0, 0)),
            pl.BlockSpec((groups, chunk, chunk), lambda i: (0, 0, 0)),
            pl.BlockSpec((chunk, groups), lambda i: (0, 0)),
            pl.BlockSpec((1, half), lambda i: (0, 0)),
        ],
        out_specs=pl.BlockSpec((tm, half), lambda i: (i, 0)),
        out_shape=jax.ShapeDtypeStruct((m, half), F32),
        compiler_params=_params("parallel"),
        name="gmlp_mix",
    )(p, p, v_gain, ws, bs_t, out_gain)


def _conv_kernel(seq_len, p_ref, prev_ref, cw_ref, og_ref, o_ref):
    tm = p_ref.shape[0]
    c = o_ref.shape[1]
    p = p_ref[...]
    gb, z = p[:, :c], p[:, c:2 * c] * p[:, 2 * c:]
    pv = prev_ref[...]
    seq_start = (pl.program_id(0) * tm) % seq_len == 0
    zp = jnp.where(seq_start, 0.0, pv[:, c:2 * c] * pv[:, 2 * c:])
    cw = cw_ref[...]
    taps = cw.shape[0]
    zc = cw[taps - 1:taps, :] * z
    for s in range(1, taps):
        zc = zc + cw[taps - 1 - s:taps - s, :] * _shift_rows(z, zp, s)
    y = gb * zc
    ms = _seg_sum(y * y, _group_ones(CONV_GROUP_W)) * (1.0 / CONV_GROUP_W)
    o_ref[...] = y * lax.rsqrt(ms + RMS_EPS) * og_ref[...]


def _short_conv(p, col0, conv_w, out_gain, seq_len):
    m = p.shape[0]
    c = out_gain.shape[1]
    n3 = 3 * c
    cb = col0 // n3
    tm = min(256, m)
    rb = tm // SUBLANES
    return pl.pallas_call(
        functools.partial(_conv_kernel, seq_len),
        grid=(m // tm,),
        in_specs=[
            pl.BlockSpec((tm, n3), lambda i: (i, cb)),
            pl.BlockSpec((SUBLANES, n3), lambda i: (jnp.maximum(i * rb - 1, 0), cb)),
            pl.BlockSpec(conv_w.shape, lambda i: (0, 0)),
            pl.BlockSpec((1, c), lambda i: (0, 0)),
        ],
        out_specs=pl.BlockSpec((tm, c), lambda i: (i, 0)),
        out_shape=jax.ShapeDtypeStruct((m, c), F32),
        compiler_params=_params("parallel"),
        name="short_conv_mix",
    )(p, p, conv_w, out_gain)


def _rwkv_prep_kernel(seq_len, n_decay, p_ref, prev_ref, mu_ref, wwa_ref, gup_ref, w0_ref, a0_ref, kk_ref,
                      ka_ref, r_o, k_o, v_o, lw_o, a_o, b_o, g_o):
    tm = p_ref.shape[0]
    c = r_o.shape[1]
    p = p_ref[...]
    seq_start = (pl.program_id(0) * tm) % seq_len == 0
    pv = jnp.where(seq_start, 0.0, prev_ref[...])
    ps = p + (_shift_rows(p, pv, 1) - p) * mu_ref[...]
    r, k, v = ps[:, :c], ps[:, c:2 * c], ps[:, 2 * c:3 * c]
    lo = ps[:, 3 * c:3 * c + wwa_ref.shape[0]]
    gd = ps[:, 3 * c + wwa_ref.shape[0]:]
    lane = lax.broadcasted_iota(jnp.int32, lo.shape, 1)
    act = jnp.where(lane < n_decay, jnp.tanh(lo), lo)
    wa = _dot(act.astype(BF16), wwa_ref[...])
    zw = w0_ref[...] + wa[:, :c]
    w = -(jnp.maximum(-zw, 0.0) + jnp.log(1.0 + jnp.exp(-jnp.abs(zw)))) - 0.5
    a = 1.0 / (1.0 + jnp.exp(-(a0_ref[...] + wa[:, c:])))
    g = _dot((1.0 / (1.0 + jnp.exp(-gd))).astype(BF16), gup_ref[...])
    kkf = k * kk_ref[...]
    nrm = jnp.sqrt(_seg_sum(kkf * kkf, _group_ones(RWKV_HEAD)))
    kk = kkf / jnp.maximum(nrm, 1e-12)
    r_o[...] = r
    k_o[...] = k * (1.0 + (a - 1.0) * ka_ref[...])
    v_o[...] = v
    lw_o[...] = -jnp.exp(w)
    a_o[...] = -kk
    b_o[...] = kk * a
    g_o[...] = g


def _rwkv_prep(pb, mu, wwa, g_up, w0, a0, k_k, k_a, seq_len, n_decay):
    m = pb.shape[0]
    nb = mu.shape[1]
    c = w0.shape[1]
    tm = min(256, m)
    rb = tm // SUBLANES
    vec = lambda n: pl.BlockSpec((1, n), lambda i: (0, 0))
    full = lambda a: pl.BlockSpec(a.shape, lambda i: (0, 0))
    out = jax.ShapeDtypeStruct((m, c), F32)
    return pl.pallas_call(
        functools.partial(_rwkv_prep_kernel, seq_len, n_decay),
        grid=(m // tm,),
        in_specs=[
            pl.BlockSpec((tm, nb), lambda i: (i, 0)),
            pl.BlockSpec((SUBLANES, nb), lambda i: (jnp.maximum(i * rb - 1, 0), 0)),
            vec(nb), full(wwa), full(g_up), vec(c), vec(c), vec(c), vec(c),
        ],
        out_specs=[pl.BlockSpec((tm, c), lambda i: (i, 0))] * 7,
        out_shape=[out] * 7,
        compiler_params=_params("parallel"),
        name="rwkv_prep",
    )(pb, pb, mu, wwa, g_up, w0, a0, k_k, k_a)


def _wkv_kernel(r_ref, k_ref, v_ref, lw_ref, a_ref, b_ref, g_ref, lnw_ref, lnb_ref, rk_ref, o_ref, h_ref):
    @pl.when(pl.program_id(1) == 0)
    def _():
        h_ref[...] = jnp.zeros_like(h_ref)

    L, C = r_ref.shape
    r, k, v, lw, a, b = r_ref[...], k_ref[...], v_ref[...], lw_ref[...], a_ref[...], b_ref[...]
    row = lax.broadcasted_iota(jnp.int32, (L, L), 0)
    col = lax.broadcasted_iota(jnp.int32, (L, L), 1)
    lower = col <= row
    strict = col < row
    eye = jnp.where(col == row, 1.0, 0.0)
    cum = _dot_hi(jnp.where(lower, 1.0, 0.0), lw)
    c_mid = cum[L // 2 - 1:L // 2, :]
    c_end = cum[L - 1:L, :]
    e_fwd = jnp.exp(cum - c_mid)
    e_bwd = jnp.exp(c_mid - cum)
    e_end = jnp.exp(c_end - cum)
    bf = lambda z: z.astype(BF16)
    rt = bf(r * e_fwd)
    at = bf(a * jnp.exp(cum - lw - c_mid))
    bt = bf(b * e_bwd)
    kt = bf(k * e_bwd)
    bh = bf(b * e_end)
    kh = bf(k * e_end)
    vb = bf(v)
    dec = jnp.exp(c_end)
    dec_mid = jnp.exp(c_mid)

    lane = lax.broadcasted_iota(jnp.int32, (1, LANES), 1)
    hi = lax.broadcasted_iota(jnp.int32, (LANES, LANES), 0) // RWKV_HEAD
    hj = lax.broadcasted_iota(jnp.int32, (LANES, LANES), 1) // RWKV_HEAD
    same_head = hi == hj
    zeros_t = jnp.zeros((L, LANES), F32)
    zero_b = jnp.zeros((), BF16)
    same_blk = lambda size: (row // size) == (col // size)

    tiles = range(C // LANES)
    per_tile = LANES // RWKV_HEAD
    heads = [(t, hh) for t in tiles for hh in range(per_tile)]
    tile_sl = [slice(t * LANES, (t + 1) * LANES) for t in tiles]
    head_mask = [(lane // RWKV_HEAD) == hh for hh in range(per_tile)]
    am = [jnp.where(head_mask[hh], at[:, tile_sl[t]], zero_b) for t, hh in heads]
    vm = [jnp.where(head_mask[hh], vb[:, tile_sl[t]], zero_b) for t, hh in heads]
    rhs = [jnp.concatenate([bt[:, tile_sl[t]], kt[:, tile_sl[t]]], axis=0) for t in tiles]
    s = [_dot_nt(jnp.concatenate([am[i], jnp.where(head_mask[hh], rt[:, tile_sl[t]], zero_b)], axis=0), rhs[t])
         for i, (t, hh) in enumerate(heads)]
    n_ab = [jnp.where(strict, si[:L, :L], 0.0) for si in s]
    a_ak = [bf(jnp.where(strict, si[:L, L:], 0.0)) for si in s]
    m_rb = [bf(jnp.where(lower, si[L:, :L], 0.0)) for si in s]
    m_rk = [bf(jnp.where(lower, si[L:, L:], 0.0)) for si in s]
    tinv = [eye + jnp.where(same_blk(2), n, 0.0) for n in n_ab]
    size = 2
    while size < L:
        sub_diag = same_blk(2 * size) & jnp.logical_not(same_blk(size))
        e = [bf(jnp.where(sub_diag, n, 0.0)) for n in n_ab]
        tb = [bf(ti) for ti in tinv]
        te = [bf(_dot(tb_, e_)) for tb_, e_ in zip(tb, e)]
        tinv = [ti + _dot(te_, tb_) for ti, te_, tb_ in zip(tinv, te, tb)]
        size *= 2
    w = [bf(_dot(ak, v_)) for ak, v_ in zip(a_ak, vm)]
    x = [_dot(bf(ti), jnp.concatenate([a_, w_], axis=1)) for ti, a_, w_ in zip(tinv, am, w)]
    zz = [_dot(mb, bf(x_)) for mb, x_ in zip(m_rb, x)]
    yk = [_dot(mk, v_) for mk, v_ in zip(m_rk, vm)]

    def tile_sum(vals, t):
        out = vals[t * per_tile]
        for j in range(1, per_tile):
            out = out + vals[t * per_tile + j]
        return out

    a_acc = [tile_sum([x_[:, :LANES] for x_ in x], t) for t in tiles]
    u_acc = [tile_sum([x_[:, LANES:] for x_ in x], t) for t in tiles]
    r_acc = [rt[:, tile_sl[t]].astype(F32) + tile_sum([z_[:, :LANES] for z_ in zz], t) for t in tiles]
    y_acc = [tile_sum([z_[:, LANES:] + y_ for z_, y_ in zip(zz, yk)], t) for t in tiles]
    h = [h_ref[t] for t in tiles]
    h_mid = [bf(h[t] * jnp.transpose(jnp.broadcast_to(dec_mid[:, tile_sl[t]], (LANES, LANES)))) for t in tiles]
    ah = [_dot(bf(jnp.concatenate([a_acc[t], r_acc[t]], axis=0)), h_mid[t]) for t in tiles]
    u = [ah[t][:L] + u_acc[t] for t in tiles]
    upd = [_dot_tn(jnp.concatenate([bh[:, tile_sl[t]], kh[:, tile_sl[t]]], axis=0),
                   jnp.concatenate([bf(u[t]), vb[:, tile_sl[t]]], axis=0)) for t in tiles]
    for t in tiles:
        dec_rows = jnp.transpose(jnp.broadcast_to(dec[:, tile_sl[t]], (LANES, LANES)))
        h_ref[t] = h[t] * dec_rows + jnp.where(same_head, upd[t], 0.0)

    y = jnp.concatenate([ah[t][L:] + y_acc[t] for t in tiles], axis=1)
    ones_bd = _group_ones(RWKV_HEAD)
    inv_n = 1.0 / RWKV_HEAD
    mean = _seg_sum(y, ones_bd) * inv_n
    d = y - mean
    var = _seg_sum(d * d, ones_bd) * inv_n
    yn = d * lax.rsqrt(var + LNX_EPS) * lnw_ref[...] + lnb_ref[...]
    bonus = _seg_sum(r * k * rk_ref[...], ones_bd) * v
    o_ref[...] = (yn + bonus) * g_ref[...]


def _wkv(r, k, v, lw, a, b, g, ln_w, ln_b, r_k, batch, seq_len):
    m, c = r.shape
    L = min(WKV_CHUNK, seq_len)
    assert L & (L - 1) == 0 and seq_len % L == 0, "chunk length must be a power of two dividing the sequence"
    nc = seq_len // L
    blk = pl.BlockSpec((L, c), lambda bi, ci: (bi * nc + ci, 0))
    vec = pl.BlockSpec((1, c), lambda bi, ci: (0, 0))
    return pl.pallas_call(
        _wkv_kernel,
        grid=(batch, nc),
        in_specs=[blk] * 7 + [vec] * 3,
        out_specs=blk,
        out_shape=jax.ShapeDtypeStruct((m, c), F32),
        scratch_shapes=[pltpu.VMEM((c // LANES, LANES, LANES), F32)],
        compiler_params=_params("parallel", "arbitrary"),
        name="wkv7_scan",
    )(r, k, v, lw, a, b, g, ln_w, ln_b, r_k)


def _out_proj_kernel(ya_ref, yb_ref, yc_ref, wa_ref, wb_ref, wc_ref, g_ref, x_ref, o_ref):
    y = (_dot(ya_ref[...].astype(BF16), wa_ref[...]) + _dot(yb_ref[...].astype(BF16), wb_ref[...])
         + _dot(yc_ref[...].astype(BF16), wc_ref[...]))
    o_ref[...] = x_ref[...] + _rms(y, g_ref[...])


def _out_proj(ya, yb, yc, wa, wb, wc, gain, x):
    m, d = x.shape
    tm = min(256, m)
    act = lambda a: pl.BlockSpec((tm, a.shape[1]), lambda i: (i, 0))
    full = lambda a: pl.BlockSpec(a.shape, lambda i: (0, 0))
    return pl.pallas_call(
        _out_proj_kernel,
        grid=(m // tm,),
        in_specs=[act(ya), act(yb), act(yc), full(wa), full(wb), full(wc), full(gain), act(x)],
        out_specs=act(x),
        out_shape=jax.ShapeDtypeStruct((m, d), F32),
        compiler_params=_params("parallel"),
        name="out_proj",
    )(ya, yb, yc, wa, wb, wc, gain, x)


def _mlp_kernel(x_ref, gpre_ref, wu_ref, wd_ref, gpost_ref, o_ref, h_ref):
    j = pl.program_id(1)

    @pl.when(j == 0)
    def _():
        h_ref[...] = _rms(x_ref[...], gpre_ref[...]).astype(BF16)
        o_ref[...] = jnp.zeros_like(o_ref)

    f = jnp.maximum(_dot(h_ref[...], wu_ref[...].astype(BF16)), 0.0)
    o_ref[...] += _dot((f * f).astype(BF16), wd_ref[...].astype(BF16))

    @pl.when(j == pl.num_programs(1) - 1)
    def _():
        o_ref[...] = x_ref[...] + _rms(o_ref[...], gpost_ref[...])


def _mlp(x, g_pre, w_up_all, w_down_all, g_post, layer):
    m, d = x.shape
    ff = w_up_all.shape[2]
    tm = min(1024, m)
    tf = min(512, ff)
    return pl.pallas_call(
        _mlp_kernel,
        grid=(m // tm, ff // tf),
        in_specs=[
            pl.BlockSpec((tm, d), lambda i, j: (i, 0), pipeline_mode=pl.Buffered(1)),
            pl.BlockSpec((1, d), lambda i, j: (0, 0)),
            pl.BlockSpec((None, d, tf), lambda i, j: (layer, 0, j)),
            pl.BlockSpec((None, tf, d), lambda i, j: (layer, j, 0)),
            pl.BlockSpec((1, d), lambda i, j: (0, 0)),
        ],
        out_specs=pl.BlockSpec((tm, d), lambda i, j: (i, 0)),
        out_shape=jax.ShapeDtypeStruct((m, d), F32),
        scratch_shapes=[pltpu.VMEM((tm, d), BF16)],
        compiler_params=_params("parallel", "arbitrary"),
        name="relu2_mlp",
    )(x, g_pre, w_up_all, w_down_all, g_post)


def kernel(x, norm_mix_pre, norm_mix_post, norm_mlp_pre, norm_mlp_post, w_in, gm_v_gain, gm_ws, gm_bs,
           gm_out_gain, rk_mu, rk_w0, rk_w_up, rk_a0, rk_a_up, rk_g_up, rk_k_k, rk_k_a, rk_r_k, rk_ln_w,
           rk_ln_b, sc_conv, sc_out_gain, w_out, mlp_up, mlp_down):
    batch, seq_len, d = x.shape
    depth = w_in.shape[0]
    mix_a = gm_v_gain.shape[1]
    mix_b = rk_w0.shape[1]
    mix_c = sc_out_gain.shape[1]
    n_decay, n_iclr, n_gate = rk_w_up.shape[1], rk_a_up.shape[1], rk_g_up.shape[1]
    n_lora = n_decay + n_iclr
    lora_pad = -(-n_lora // LANES) * LANES
    col_b = 2 * mix_a
    col_lora = col_b + 3 * mix_b
    col_gate = col_lora + n_lora
    col_c = col_gate + n_gate
    row = lambda a: a.reshape(1, -1)

    xf = x.reshape(batch * seq_len, d)
    w_proj = jnp.concatenate([w_in[:, :, col_b:col_gate], jnp.zeros((depth, d, lora_pad - n_lora), F32),
                              w_in[:, :, col_gate:col_c], w_in[:, :, :col_b], w_in[:, :, col_c:]],
                             axis=2).astype(BF16)
    col_gmlp = 3 * mix_b + lora_pad + n_gate
    col_conv = col_gmlp + 2 * mix_a
    for l in range(depth):
        mu = rk_mu[l]
        mu_b = jnp.concatenate([mu[:3 * mix_b], jnp.pad(mu[3 * mix_b:3 * mix_b + n_lora], (0, lora_pad - n_lora)),
                                mu[3 * mix_b + n_lora:]]).reshape(1, -1)
        wwa = jnp.zeros((lora_pad, 2 * mix_b), F32)
        wwa = wwa.at[:n_decay, :mix_b].set(rk_w_up[l]).at[n_decay:n_lora, mix_b:].set(rk_a_up[l]).astype(BF16)

        p = _norm_matmul(xf, row(norm_mix_pre[l]), w_proj, l)

        ya = _gmlp(p, col_gmlp, row(gm_v_gain[l]), gm_ws[l], gm_bs[l].T, row(gm_out_gain[l]))
        scan_in = _rwkv_prep(p, mu_b, wwa, rk_g_up[l].astype(BF16), row(rk_w0[l]), row(rk_a0[l]),
                             row(rk_k_k[l]), row(rk_k_a[l]), seq_len, n_decay)
        yb = _wkv(*scan_in, row(rk_ln_w[l]), row(rk_ln_b[l]), row(rk_r_k[l]), batch, seq_len)
        yc = _short_conv(p, col_conv, sc_conv[l], row(sc_out_gain[l]), seq_len)

        wo = w_out[l].astype(BF16)
        xf = _out_proj(ya, yb, yc, wo[:mix_a], wo[mix_a:mix_a + mix_b], wo[mix_a + mix_b:],
                       row(norm_mix_post[l]), xf)
        xf = _mlp(xf, row(norm_mlp_pre[l]), mlp_up, mlp_down, row(norm_mlp_post[l]), l)
    return xf.reshape(batch, seq_len, d)
```

```python
import functools

import jax
import jax.numpy as jnp
from jax import lax
from jax.experimental import pallas as pl
from jax.experimental.pallas import tpu as pltpu

F32 = jnp.float32
BF16 = jnp.bfloat16
HIGHEST = lax.Precision.HIGHEST

RMS_EPS = 1e-6
LNX_EPS = 64e-5
RWKV_HEAD = 64
CONV_GROUP_W = 64
LANES = 128
SUBLANES = 8
WKV_STEP = 128
VMEM_LIMIT = 56 * 1024 * 1024


def _params(*sem):
    return pltpu.CompilerParams(dimension_semantics=sem, vmem_limit_bytes=VMEM_LIMIT)


def _rms(x, gain):
    return x * lax.rsqrt(jnp.mean(x * x, axis=-1, keepdims=True) + RMS_EPS) * gain


def _dot(a, b):
    return jnp.dot(a, b, preferred_element_type=F32)


def _dot_hi(a, b):
    return jnp.dot(a, b, preferred_element_type=F32, precision=HIGHEST)


def _dot_nt(a, b):
    return lax.dot_general(a, b, (((1,), (1,)), ((), ())), preferred_element_type=F32)


def _dot_tn(a, b):
    return lax.dot_general(a, b, (((0,), (0,)), ((), ())), preferred_element_type=F32)


def _group_ones(width):
    i = lax.broadcasted_iota(jnp.int32, (LANES, LANES), 0) // width
    j = lax.broadcasted_iota(jnp.int32, (LANES, LANES), 1) // width
    return jnp.where(i == j, 1.0, 0.0).astype(BF16)


def _seg_sum(x, ones_bd):
    hi = x.astype(BF16)
    lo = (x - hi.astype(F32)).astype(BF16)
    cols = []
    for c in range(x.shape[-1] // LANES):
        sl = slice(c * LANES, (c + 1) * LANES)
        cols.append(_dot(hi[:, sl], ones_bd) + _dot(lo[:, sl], ones_bd))
    return cols[0] if len(cols) == 1 else jnp.concatenate(cols, axis=-1)


def _shift_rows(x, prev_rows, n):
    rolled = pltpu.roll(x, n, axis=0)
    row = lax.broadcasted_iota(jnp.int32, x.shape, 0)
    p = prev_rows.shape[0]
    for j in range(n):
        rolled = jnp.where(row == j, prev_rows[p - n + j:p - n + j + 1, :], rolled)
    return rolled


def _norm_matmul_kernel(x_ref, g_ref, w_ref, o_ref, h_ref):
    @pl.when(pl.program_id(1) == 0)
    def _():
        h_ref[...] = _rms(x_ref[...], g_ref[...]).astype(BF16)

    o_ref[...] = _dot(h_ref[...], w_ref[...])


def _norm_matmul(x, gain, w_all, layer):
    m, d = x.shape
    n = w_all.shape[2]
    tm = min(1024, m)
    tn = 1024
    return pl.pallas_call(
        _norm_matmul_kernel,
        grid=(m // tm, n // tn),
        in_specs=[
            pl.BlockSpec((tm, d), lambda i, j: (i, 0)),
            pl.BlockSpec((1, d), lambda i, j: (0, 0)),
            pl.BlockSpec((None, d, tn), lambda i, j: (layer, 0, j)),
        ],
        out_specs=pl.BlockSpec((tm, tn), lambda i, j: (i, j)),
        out_shape=jax.ShapeDtypeStruct((m, n), F32),
        scratch_shapes=[pltpu.VMEM((tm, d), BF16)],
        compiler_params=_params("parallel", "arbitrary"),
        name="in_proj",
    )(x, gain, w_all)


def _gelu_tanh(x):
    return 0.5 * x * (1.0 + jnp.tanh(0.7978845608028654 * (x + 0.044715 * (x * x * x))))


def _gmlp_kernel(pu_ref, pv_ref, vg_ref, ws_ref, bst_ref, og_ref, o_ref):
    groups, chunk, _ = ws_ref.shape
    tm = pu_ref.shape[0]
    zu = _gelu_tanh(pu_ref[...])
    zv = _gelu_tanh(pv_ref[...])
    row = lax.broadcasted_iota(jnp.int32, (chunk, chunk), 0)
    col = lax.broadcasted_iota(jnp.int32, (chunk, chunk), 1)
    bst = bst_ref[...]
    for g in range(groups):
        sl = slice(g * LANES, (g + 1) * LANES)
        u = zu[:, sl]
        v = _rms(zv[:, sl], vg_ref[:, sl]).astype(BF16)
        w = jnp.where(col <= row, ws_ref[g], 0.0).astype(BF16)
        bias = bst[:, g:g + 1]
        for c in range(tm // chunk):
            rows = slice(c * chunk, (c + 1) * chunk)
            mixed = _dot(w, v[rows]) + bias
            o_ref[rows, sl] = _rms(u[rows] * mixed, og_ref[:, sl])


def _gmlp(p, col0, v_gain, ws, bs_t, out_gain):
    m = p.shape[0]
    groups, chunk, _ = ws.shape
    half = v_gain.shape[1]
    cb = col0 // half
    tm = min(256, m)
    return pl.pallas_call(
        _gmlp_kernel,
        grid=(m // tm,),
        in_specs=[
            pl.BlockSpec((tm, half), lambda i: (i, cb)),
            pl.BlockSpec((tm, half), lambda i: (i, cb + 1)),
            pl.BlockSpec((1, half), lambda i: (0, 0)),
            pl.BlockSpec((groups, chunk, chunk), lambda i: (0, 0, 0)),
            pl.BlockSpec((chunk, groups), lambda i: (0, 0)),
            pl.BlockSpec((1, half), lambda i: (0, 0)),
        ],
        out_specs=pl.BlockSpec((tm, half), lambda i: (i, 0)),
        out_shape=jax.ShapeDtypeStruct((m, half), F32),
        compiler_params=_params("parallel"),
        name="gmlp_mix",
    )(p, p, v_gain, ws, bs_t, out_gain)


def _conv_kernel(seq_len, p_ref, prev_ref, cw_ref, og_ref, o_ref):
    tm = p_ref.shape[0]
    c = o_ref.shape[1]
    p = p_ref[...]
    gb, z = p[:, :c], p[:, c:2 * c] * p[:, 2 * c:]
    pv = prev_ref[...]
    seq_start = (pl.program_id(0) * tm) % seq_len == 0
    zp = jnp.where(seq_start, 0.0, pv[:, c:2 * c] * pv[:, 2 * c:])
    cw = cw_ref[...]
    taps = cw.shape[0]
    zc = cw[taps - 1:taps, :] * z
    for s in range(1, taps):
        zc = zc + cw[taps - 1 - s:taps - s, :] * _shift_rows(z, zp, s)
    y = gb * zc
    ms = _seg_sum(y * y, _group_ones(CONV_GROUP_W)) * (1.0 / CONV_GROUP_W)
    o_ref[...] = y * lax.rsqrt(ms + RMS_EPS) * og_ref[...]


def _short_conv(p, col0, conv_w, out_gain, seq_len):
    m = p.shape[0]
    c = out_gain.shape[1]
    n3 = 3 * c
    cb = col0 // n3
    tm = min(256, m)
    rb = tm // SUBLANES
    return pl.pallas_call(
        functools.partial(_conv_kernel, seq_len),
        grid=(m // tm,),
        in_specs=[
            pl.BlockSpec((tm, n3), lambda i: (i, cb)),
            pl.BlockSpec((SUBLANES, n3), lambda i: (jnp.maximum(i * rb - 1, 0), cb)),
            pl.BlockSpec(conv_w.shape, lambda i: (0, 0)),
            pl.BlockSpec((1, c), lambda i: (0, 0)),
        ],
        out_specs=pl.BlockSpec((tm, c), lambda i: (i, 0)),
        out_shape=jax.ShapeDtypeStruct((m, c), F32),
        compiler_params=_params("parallel"),
        name="short_conv_mix",
    )(p, p, conv_w, out_gain)


def _rwkv_operands(p, prev_row, mu, wwa, g_up, w0, a0, k_k, k_a, n_decay):
    c = w0.shape[1]
    ps = p + (_shift_rows(p, prev_row, 1) - p) * mu
    r, k, v = ps[:, :c], ps[:, c:2 * c], ps[:, 2 * c:3 * c]
    lo = ps[:, 3 * c:3 * c + wwa.shape[0]]
    gd = ps[:, 3 * c + wwa.shape[0]:]
    lane = lax.broadcasted_iota(jnp.int32, lo.shape, 1)
    act = jnp.where(lane < n_decay, jnp.tanh(lo), lo)
    wa = _dot(act.astype(BF16), wwa)
    zw = w0 + wa[:, :c]
    w = -(jnp.maximum(-zw, 0.0) + jnp.log(1.0 + jnp.exp(-jnp.abs(zw)))) - 0.5
    iclr = 1.0 / (1.0 + jnp.exp(-(a0 + wa[:, c:])))
    g = _dot((1.0 / (1.0 + jnp.exp(-gd))).astype(BF16), g_up)
    kkf = k * k_k
    nrm = jnp.sqrt(_seg_sum(kkf * kkf, _group_ones(RWKV_HEAD)))
    kk = kkf / jnp.maximum(nrm, 1e-12)
    return r, k * (1.0 + (iclr - 1.0) * k_a), v, -jnp.exp(w), -kk, kk * iclr, g


def _wkv_kernel(n_decay, p_ref, mu_ref, wwa_ref, gup_ref, w0_ref, a0_ref, kk_ref, ka_ref, lnw_ref, lnb_ref,
                rk_ref, o_ref, h_ref, prev_ref):
    @pl.when(pl.program_id(1) == 0)
    def _():
        h_ref[...] = jnp.zeros_like(h_ref)
        prev_ref[...] = jnp.zeros_like(prev_ref)

    TB, C = o_ref.shape
    L = RWKV_HEAD
    NCH = TB // L
    HPT = LANES // RWKV_HEAD
    NB = NCH * HPT
    p = p_ref[...]
    r, k, v, lw, a, b, g = _rwkv_operands(p, prev_ref[0:1, :], mu_ref[...], wwa_ref[...], gup_ref[...], w0_ref[...],
                                          a0_ref[...], kk_ref[...], ka_ref[...], n_decay)
    prev_ref[0:1, :] = p[TB - 1:TB, :]
    row = lax.broadcasted_iota(jnp.int32, (TB, TB), 0)
    col = lax.broadcasted_iota(jnp.int32, (TB, TB), 1)
    cum = _dot_hi(jnp.where((col <= row) & (row // L == col // L), 1.0, 0.0), lw)

    def chunk_rows(i):
        return jnp.concatenate([jnp.broadcast_to(cum[c * L + i:c * L + i + 1], (L, C)) for c in range(NCH)], axis=0)

    c_mid = chunk_rows(L // 2 - 1)
    c_end = chunk_rows(L - 1)
    e_fwd = jnp.exp(cum - c_mid)
    e_bwd = jnp.exp(c_mid - cum)
    e_end = jnp.exp(c_end - cum)
    bf = lambda z: z.astype(BF16)
    rt = bf(r * e_fwd)
    at = bf(a * jnp.exp(cum - lw - c_mid))
    bt = bf(b * e_bwd)
    kt = bf(k * e_bwd)
    bh = bf(b * e_end)
    kh = bf(k * e_end)
    vb = bf(v)

    lane = lax.broadcasted_iota(jnp.int32, (1, LANES), 1)
    hi = lax.broadcasted_iota(jnp.int32, (LANES, LANES), 0) // RWKV_HEAD
    hj = lax.broadcasted_iota(jnp.int32, (LANES, LANES), 1) // RWKV_HEAD
    same_head = hi == hj
    zero_b = jnp.zeros((), BF16)
    zeros_lt = jnp.zeros((L, LANES), BF16)

    tiles = range(C // LANES)
    chunks = range(NCH)
    tile_sl = [slice(t * LANES, (t + 1) * LANES) for t in tiles]
    rows = [slice(c * L, (c + 1) * L) for c in chunks]
    head_mask = [(lane // RWKV_HEAD) == h for h in range(HPT)]
    head_mask2 = [jnp.concatenate([m, m], axis=1) for m in head_mask]
    low_half = lane < RWKV_HEAD
    lr = lax.broadcasted_iota(jnp.int32, (L, LANES), 1) % L
    rr = lax.broadcasted_iota(jnp.int32, (L, LANES), 0)
    strict = lr < rr
    lower = lr <= rr
    lrp = lax.broadcasted_iota(jnp.int32, (L, NB * L), 1) % L
    rrp = lax.broadcasted_iota(jnp.int32, (L, NB * L), 0)
    blkp = lax.broadcasted_iota(jnp.int32, (L, NB * L), 1) // L
    same_blk = lambda size: (lrp // size) == (rrp // size)

    def block_diag(xp):
        return jnp.concatenate([jnp.where(blkp == q, xp, zero_b) for q in range(NB)], axis=0)

    def masked(x, t, c, h):
        return jnp.where(head_mask[h], x[rows[c], tile_sl[t]], zero_b)

    am = [[[masked(at, t, c, h) for h in range(HPT)] for c in chunks] for t in tiles]
    vm = [[[masked(vb, t, c, h) for h in range(HPT)] for c in chunks] for t in tiles]

    def score(t, c, h):
        b_, k_ = bt[rows[c], tile_sl[t]], kt[rows[c], tile_sl[t]]
        rhs = jnp.concatenate([b_, k_] if h == 0 else [k_, b_], axis=0)
        return _dot_nt(jnp.concatenate([am[t][c][h], masked(rt, t, c, h)], axis=0), rhs)

    s = [[[score(t, c, h) for h in range(HPT)] for c in chunks] for t in tiles]

    def pick(t, c, part, first):
        e, o = s[t][c][0][part * L:(part + 1) * L], s[t][c][1][part * L:(part + 1) * L]
        return jnp.where(low_half, e, o) if first == 0 else jnp.where(low_half, o, e)

    n_p = [jnp.concatenate([jnp.where(strict, pick(t, c, 0, 0), 0.0) for c in chunks], axis=1) for t in tiles]
    ak_p = [jnp.concatenate([bf(jnp.where(strict, pick(t, c, 0, 1), 0.0)) for c in chunks], axis=1) for t in tiles]
    rb = [[bf(jnp.where(lower, pick(t, c, 1, 0), 0.0)) for c in chunks] for t in tiles]
    rk = [[bf(jnp.where(lower, pick(t, c, 1, 1), 0.0)) for c in chunks] for t in tiles]
    eye_p = jnp.where(lrp == rrp, 1.0, 0.0)
    tinv = [eye_p + jnp.where(same_blk(2), n, 0.0) for n in n_p]
    size = 2
    while size < L:
        sub_diag = same_blk(2 * size) & jnp.logical_not(same_blk(size))
        e_bd = [block_diag(bf(jnp.where(sub_diag, n, 0.0))) for n in n_p]
        tb = [bf(ti) for ti in tinv]
        te = [bf(_dot(tb_, e_)) for tb_, e_ in zip(tb, e_bd)]
        tinv = [ti + _dot(te_, block_diag(tb_)) for ti, te_, tb_ in zip(tinv, te, tb)]
        size *= 2
    tb = [bf(ti) for ti in tinv]

    def place(x, c):
        return jnp.concatenate([x if cc == c else zeros_lt for cc in chunks], axis=1)

    w = [_dot(ak_p[t], jnp.concatenate([place(vm[t][c][h], c) for c in chunks for h in (1, 0)], axis=0))
         for t in tiles]

    def solve(t, c):
        wc = bf(w[t][:, c * LANES:(c + 1) * LANES])
        rhs = jnp.concatenate([jnp.concatenate([am[t][c][h], jnp.where(head_mask[h], wc, zero_b)], axis=1)
                               for h in range(HPT)], axis=0)
        return _dot(tb[t][:, c * LANES:(c + 1) * LANES], rhs)

    x = [[solve(t, c) for c in chunks] for t in tiles]

    def readout(t, c):
        xb = bf(x[t][c])
        rhs = jnp.concatenate([jnp.where(head_mask2[0], xb, zero_b), jnp.where(head_mask2[1], xb, zero_b),
                               jnp.concatenate([zeros_lt, vm[t][c][1]], axis=1),
                               jnp.concatenate([zeros_lt, vm[t][c][0]], axis=1)], axis=0)
        return _dot(jnp.concatenate([rb[t][c], rk[t][c]], axis=1), rhs)

    z = [[readout(t, c) for c in chunks] for t in tiles]

    h = [h_ref[t] for t in tiles]
    y_rows = [[None] * NCH for _ in tiles]
    for c in chunks:
        dec_mid = jnp.exp(cum[c * L + L // 2 - 1:c * L + L // 2, :])
        dec = jnp.exp(cum[c * L + L - 1:c * L + L, :])
        h_mid = [bf(h[t] * jnp.transpose(jnp.broadcast_to(dec_mid[:, tile_sl[t]], (LANES, LANES)))) for t in tiles]
        lhs = [bf(jnp.concatenate([x[t][c][:, :LANES],
                                   rt[rows[c], tile_sl[t]].astype(F32) + z[t][c][:, :LANES]], axis=0)) for t in tiles]
        ah = [_dot(lhs[t], h_mid[t]) for t in tiles]
        u = [ah[t][:L] + x[t][c][:, LANES:] for t in tiles]
        for t in tiles:
            y_rows[t][c] = ah[t][L:] + z[t][c][:, LANES:]
        upd = [_dot_tn(jnp.concatenate([bh[rows[c], tile_sl[t]], kh[rows[c], tile_sl[t]]], axis=0),
                       jnp.concatenate([bf(u[t]), vb[rows[c], tile_sl[t]]], axis=0)) for t in tiles]
        h = [h[t] * jnp.transpose(jnp.broadcast_to(dec[:, tile_sl[t]], (LANES, LANES)))
             + jnp.where(same_head, upd[t], 0.0) for t in tiles]
    for t in tiles:
        h_ref[t] = h[t]

    y = jnp.concatenate([jnp.concatenate(y_rows[t], axis=0) for t in tiles], axis=1)
    ones_bd = _group_ones(RWKV_HEAD)
    inv_n = 1.0 / RWKV_HEAD
    mean = _seg_sum(y, ones_bd) * inv_n
    d = y - mean
    var = _seg_sum(d * d, ones_bd) * inv_n
    yn = d * lax.rsqrt(var + LNX_EPS) * lnw_ref[...] + lnb_ref[...]
    bonus = _seg_sum(r * k * rk_ref[...], ones_bd) * v
    o_ref[...] = (yn + bonus) * g


def _rwkv_mix(p, mu, wwa, g_up, w0, a0, k_k, k_a, ln_w, ln_b, r_k, batch, seq_len, n_decay):
    m = p.shape[0]
    nb = mu.shape[1]
    c = w0.shape[1]
    tb = min(WKV_STEP, seq_len)
    assert LANES == 2 * RWKV_HEAD and RWKV_HEAD & (RWKV_HEAD - 1) == 0, "kernel packs two power-of-two heads per lane tile"
    assert tb % RWKV_HEAD == 0 and seq_len % tb == 0 and c % LANES == 0
    nc = seq_len // tb
    vec = lambda n: pl.BlockSpec((1, n), lambda bi, ci: (0, 0))
    full = lambda a: pl.BlockSpec(a.shape, lambda bi, ci: (0, 0))
    return pl.pallas_call(
        functools.partial(_wkv_kernel, n_decay),
        grid=(batch, nc),
        in_specs=[pl.BlockSpec((tb, nb), lambda bi, ci: (bi * nc + ci, 0)),
                  vec(nb), full(wwa), full(g_up)] + [vec(c)] * 7,
        out_specs=pl.BlockSpec((tb, c), lambda bi, ci: (bi * nc + ci, 0)),
        out_shape=jax.ShapeDtypeStruct((m, c), F32),
        scratch_shapes=[pltpu.VMEM((c // LANES, LANES, LANES), F32), pltpu.VMEM((SUBLANES, nb), F32)],
        compiler_params=_params("parallel", "arbitrary"),
        name="rwkv7_mix",
    )(p, mu, wwa, g_up, w0, a0, k_k, k_a, ln_w, ln_b, r_k)


def _out_proj_kernel(na, nb, ya_ref, yb_ref, yc_ref, w_ref, g_ref, x_ref, o_ref):
    kb = pl.program_id(1)

    @pl.when(kb == 0)
    def _():
        o_ref[...] = jnp.zeros_like(o_ref)

    w = w_ref[...].astype(BF16)

    def accumulate(y_ref):
        o_ref[...] += _dot(y_ref[...].astype(BF16), w)

    pl.when(kb < na)(lambda: accumulate(ya_ref))
    pl.when((kb >= na) & (kb < na + nb))(lambda: accumulate(yb_ref))
    pl.when(kb >= na + nb)(lambda: accumulate(yc_ref))

    @pl.when(kb == pl.num_programs(1) - 1)
    def _():
        o_ref[...] = x_ref[...] + _rms(o_ref[...], g_ref[...])


def _out_proj(ya, yb, yc, w_all, layer, gain, x):
    m, d = x.shape
    tm = min(512, m)
    tk = min(ya.shape[1], yb.shape[1], yc.shape[1])
    na, nb, nc = ya.shape[1] // tk, yb.shape[1] // tk, yc.shape[1] // tk
    assert (na * tk, nb * tk, nc * tk) == (ya.shape[1], yb.shape[1], yc.shape[1])
    return pl.pallas_call(
        functools.partial(_out_proj_kernel, na, nb),
        grid=(m // tm, na + nb + nc),
        in_specs=[
            pl.BlockSpec((tm, tk), lambda i, kb: (i, jnp.clip(kb, 0, na - 1))),
            pl.BlockSpec((tm, tk), lambda i, kb: (i, jnp.clip(kb - na, 0, nb - 1))),
            pl.BlockSpec((tm, tk), lambda i, kb: (i, jnp.clip(kb - na - nb, 0, nc - 1))),
            pl.BlockSpec((None, tk, d), lambda i, kb: (layer, kb, 0)),
            pl.BlockSpec((1, d), lambda i, kb: (0, 0)),
            pl.BlockSpec((tm, d), lambda i, kb: (i, 0)),
        ],
        out_specs=pl.BlockSpec((tm, d), lambda i, kb: (i, 0)),
        out_shape=jax.ShapeDtypeStruct((m, d), F32),
        compiler_params=_params("parallel", "arbitrary"),
        name="out_proj",
    )(ya, yb, yc, w_all, gain, x)


def _mlp_kernel(x_ref, gpre_ref, wu_ref, wd_ref, gpost_ref, o_ref, h_ref):
    j = pl.program_id(1)

    @pl.when(j == 0)
    def _():
        h_ref[...] = _rms(x_ref[...], gpre_ref[...]).astype(BF16)
        o_ref[...] = jnp.zeros_like(o_ref)

    f = jnp.maximum(_dot(h_ref[...], wu_ref[...].astype(BF16)), 0.0)
    o_ref[...] += _dot((f * f).astype(BF16), wd_ref[...].astype(BF16))

    @pl.when(j == pl.num_programs(1) - 1)
    def _():
        o_ref[...] = x_ref[...] + _rms(o_ref[...], gpost_ref[...])


def _mlp(x, g_pre, w_up_all, w_down_all, g_post, layer):
    m, d = x.shape
    ff = w_up_all.shape[2]
    tm = min(1024, m)
    tf = min(512, ff)
    return pl.pallas_call(
        _mlp_kernel,
        grid=(m // tm, ff // tf),
        in_specs=[
            pl.BlockSpec((tm, d), lambda i, j: (i, 0), pipeline_mode=pl.Buffered(1)),
            pl.BlockSpec((1, d), lambda i, j: (0, 0)),
            pl.BlockSpec((None, d, tf), lambda i, j: (layer, 0, j)),
            pl.BlockSpec((None, tf, d), lambda i, j: (layer, j, 0)),
            pl.BlockSpec((1, d), lambda i, j: (0, 0)),
        ],
        out_specs=pl.BlockSpec((tm, d), lambda i, j: (i, 0)),
        out_shape=jax.ShapeDtypeStruct((m, d), F32),
        scratch_shapes=[pltpu.VMEM((tm, d), BF16)],
        compiler_params=_params("parallel", "arbitrary"),
        name="relu2_mlp",
    )(x, g_pre, w_up_all, w_down_all, g_post)


def kernel(x, norm_mix_pre, norm_mix_post, norm_mlp_pre, norm_mlp_post, w_in, gm_v_gain, gm_ws, gm_bs,
           gm_out_gain, rk_mu, rk_w0, rk_w_up, rk_a0, rk_a_up, rk_g_up, rk_k_k, rk_k_a, rk_r_k, rk_ln_w,
           rk_ln_b, sc_conv, sc_out_gain, w_out, mlp_up, mlp_down):
    batch, seq_len, d = x.shape
    depth = w_in.shape[0]
    mix_a = gm_v_gain.shape[1]
    mix_b = rk_w0.shape[1]
    mix_c = sc_out_gain.shape[1]
    n_decay, n_iclr, n_gate = rk_w_up.shape[1], rk_a_up.shape[1], rk_g_up.shape[1]
    n_lora = n_decay + n_iclr
    lora_pad = -(-n_lora // LANES) * LANES
    col_b = 2 * mix_a
    col_lora = col_b + 3 * mix_b
    col_gate = col_lora + n_lora
    col_c = col_gate + n_gate
    row = lambda a: a.reshape(1, -1)

    xf = x.reshape(batch * seq_len, d)
    w_proj = jnp.concatenate([w_in[:, :, col_b:col_gate], jnp.zeros((depth, d, lora_pad - n_lora), F32),
                              w_in[:, :, col_gate:col_c], w_in[:, :, :col_b], w_in[:, :, col_c:]],
                             axis=2).astype(BF16)
    col_gmlp = 3 * mix_b + lora_pad + n_gate
    col_conv = col_gmlp + 2 * mix_a
    for l in range(depth):
        mu = rk_mu[l]
        mu_b = jnp.concatenate([mu[:3 * mix_b], jnp.pad(mu[3 * mix_b:3 * mix_b + n_lora], (0, lora_pad - n_lora)),
                                mu[3 * mix_b + n_lora:]]).reshape(1, -1)
        wwa = jnp.zeros((lora_pad, 2 * mix_b), F32)
        wwa = wwa.at[:n_decay, :mix_b].set(rk_w_up[l]).at[n_decay:n_lora, mix_b:].set(rk_a_up[l]).astype(BF16)

        p = _norm_matmul(xf, row(norm_mix_pre[l]), w_proj, l)

        ya = _gmlp(p, col_gmlp, row(gm_v_gain[l]), gm_ws[l], gm_bs[l].T, row(gm_out_gain[l]))
        yb = _rwkv_mix(p, mu_b, wwa, rk_g_up[l].astype(BF16), row(rk_w0[l]), row(rk_a0[l]), row(rk_k_k[l]),
                       row(rk_k_a[l]), row(rk_ln_w[l]), row(rk_ln_b[l]), row(rk_r_k[l]), batch, seq_len, n_decay)
        yc = _short_conv(p, col_conv, sc_conv[l], row(sc_out_gain[l]), seq_len)

        xf = _out_proj(ya, yb, yc, w_out, l, row(norm_mix_post[l]), xf)
        xf = _mlp(xf, row(norm_mlp_pre[l]), mlp_up, mlp_down, row(norm_mlp_post[l]), l)
    return xf.reshape(batch, seq_len, d)
```

```python
import functools

import jax
import jax.numpy as jnp
from jax import lax
from jax.experimental import pallas as pl
from jax.experimental.pallas import tpu as pltpu

F32 = jnp.float32
BF16 = jnp.bfloat16
HIGHEST = lax.Precision.HIGHEST

RMS_EPS = 1e-6
LNX_EPS = 64e-5
RWKV_HEAD = 64
CONV_GROUP_W = 64
LANES = 128
SUBLANES = 8
WKV_STEP = 128
PROJ_TILE_N = 1024
VMEM_LIMIT = 56 * 1024 * 1024


def _params(*sem):
    return pltpu.CompilerParams(dimension_semantics=sem, vmem_limit_bytes=VMEM_LIMIT)


def _rms(x, gain):
    return x * lax.rsqrt(jnp.mean(x * x, axis=-1, keepdims=True) + RMS_EPS) * gain


def _dot(a, b):
    return jnp.dot(a, b, preferred_element_type=F32)


def _dot_hi(a, b):
    return jnp.dot(a, b, preferred_element_type=F32, precision=HIGHEST)


def _dot_nt(a, b):
    return lax.dot_general(a, b, (((1,), (1,)), ((), ())), preferred_element_type=F32)


def _dot_tn(a, b):
    return lax.dot_general(a, b, (((0,), (0,)), ((), ())), preferred_element_type=F32)


def _group_ones(width):
    i = lax.broadcasted_iota(jnp.int32, (LANES, LANES), 0) // width
    j = lax.broadcasted_iota(jnp.int32, (LANES, LANES), 1) // width
    return jnp.where(i == j, 1.0, 0.0).astype(BF16)


def _seg_sum(x, ones_bd):
    hi = x.astype(BF16)
    lo = (x - hi.astype(F32)).astype(BF16)
    cols = []
    for c in range(x.shape[-1] // LANES):
        sl = slice(c * LANES, (c + 1) * LANES)
        cols.append(_dot(hi[:, sl], ones_bd) + _dot(lo[:, sl], ones_bd))
    return cols[0] if len(cols) == 1 else jnp.concatenate(cols, axis=-1)


def _shift_rows(x, prev_rows, n):
    rolled = pltpu.roll(x, n, axis=0)
    row = lax.broadcasted_iota(jnp.int32, x.shape, 0)
    p = prev_rows.shape[0]
    for j in range(n):
        rolled = jnp.where(row == j, prev_rows[p - n + j:p - n + j + 1, :], rolled)
    return rolled


def _norm_matmul_kernel(x_ref, g_ref, w_ref, o_ref, h_ref):
    @pl.when(pl.program_id(1) == 0)
    def _():
        h_ref[...] = _rms(x_ref[...], g_ref[...]).astype(BF16)

    o_ref[...] = _dot(h_ref[...], w_ref[...])


def _norm_matmul(x, gain, w_all, layer):
    m, d = x.shape
    n = w_all.shape[2]
    tm = min(1024, m)
    tn = PROJ_TILE_N
    return pl.pallas_call(
        _norm_matmul_kernel,
        grid=(m // tm, n // tn),
        in_specs=[
            pl.BlockSpec((tm, d), lambda i, j: (i, 0)),
            pl.BlockSpec((1, d), lambda i, j: (0, 0)),
            pl.BlockSpec((None, d, tn), lambda i, j: (layer, 0, j)),
        ],
        out_specs=pl.BlockSpec((tm, tn), lambda i, j: (i, j)),
        out_shape=jax.ShapeDtypeStruct((m, n), F32),
        scratch_shapes=[pltpu.VMEM((tm, d), BF16)],
        compiler_params=_params("parallel", "arbitrary"),
        name="in_proj",
    )(x, gain, w_all)


def _gelu_tanh(x):
    return 0.5 * x * (1.0 + jnp.tanh(0.7978845608028654 * (x + 0.044715 * (x * x * x))))


def _gmlp_kernel(pu_ref, pv_ref, vg_ref, ws_ref, bst_ref, og_ref, o_ref):
    groups, chunk, _ = ws_ref.shape
    tm = pu_ref.shape[0]
    zu = _gelu_tanh(pu_ref[...])
    zv = _gelu_tanh(pv_ref[...])
    row = lax.broadcasted_iota(jnp.int32, (chunk, chunk), 0)
    col = lax.broadcasted_iota(jnp.int32, (chunk, chunk), 1)
    bst = bst_ref[...]
    for g in range(groups):
        sl = slice(g * LANES, (g + 1) * LANES)
        u = zu[:, sl]
        v = _rms(zv[:, sl], vg_ref[:, sl]).astype(BF16)
        w = jnp.where(col <= row, ws_ref[g], 0.0).astype(BF16)
        bias = bst[:, g:g + 1]
        for c in range(tm // chunk):
            rows = slice(c * chunk, (c + 1) * chunk)
            mixed = _dot(w, v[rows]) + bias
            o_ref[rows, sl] = _rms(u[rows] * mixed, og_ref[:, sl])


def _gmlp(p, col0, v_gain, ws, bs_t, out_gain):
    m = p.shape[0]
    groups, chunk, _ = ws.shape
    half = v_gain.shape[1]
    cb = col0 // half
    assert cb * half == col0
    tm = min(512, m)
    return pl.pallas_call(
        _gmlp_kernel,
        grid=(m // tm,),
        in_specs=[
            pl.BlockSpec((tm, half), lambda i: (i, cb)),
            pl.BlockSpec((tm, half), lambda i: (i, cb + 1)),
            pl.BlockSpec((1, half), lambda i: (0, 0)),
            pl.BlockSpec((groups, chunk, chunk), lambda i: (0, 0, 0)),
            pl.BlockSpec((chunk, groups), lambda i: (0, 0)),
            pl.BlockSpec((1, half), lambda i: (0, 0)),
        ],
        out_specs=pl.BlockSpec((tm, half), lambda i: (i, 0)),
        out_shape=jax.ShapeDtypeStruct((m, half), F32),
        compiler_params=_params("parallel"),
        name="gmlp_mix",
    )(p, p, v_gain, ws, bs_t, out_gain)


def _conv_kernel(seq_len, starts, p_ref, prev_ref, cw_ref, og_ref, o_ref):
    tm = p_ref.shape[0]
    fw = o_ref.shape[1]
    sg, sc, sh = starts
    p = p_ref[...]
    gb, z = p[:, sg:sg + fw], p[:, sc:sc + fw] * p[:, sh:sh + fw]
    pv = prev_ref[...]
    seq_start = (pl.program_id(0) * tm) % seq_len == 0
    zp = jnp.where(seq_start, 0.0, pv[:, sc:sc + fw] * pv[:, sh:sh + fw])
    cw = cw_ref[...]
    taps = cw.shape[0]
    zc = cw[taps - 1:taps, :] * z
    for s in range(1, taps):
        zc = zc + cw[taps - 1 - s:taps - s, :] * _shift_rows(z, zp, s)
    y = gb * zc
    ms = _seg_sum(y * y, _group_ones(CONV_GROUP_W)) * (1.0 / CONV_GROUP_W)
    o_ref[...] = y * lax.rsqrt(ms + RMS_EPS) * og_ref[...]


def _short_conv(p, col0, conv_w, out_gain, seq_len):
    m, n_all = p.shape
    c = out_gain.shape[1]
    phase = col0 % LANES
    fw = c + LANES
    win = 4 * c
    wb = col0 // win
    assert (wb + 1) * win <= n_all and c % LANES == 0 and phase % CONV_GROUP_W == 0
    starts = tuple(col0 + j * c - phase - wb * win for j in range(3))
    assert starts[0] >= 0 and starts[2] + fw <= win
    pad = lambda a: jnp.pad(a, ((0, 0), (phase, LANES - phase)))
    tm = min(512, seq_len)
    assert seq_len % tm == 0, "a row block must not straddle two sequences"
    rb = tm // SUBLANES
    return pl.pallas_call(
        functools.partial(_conv_kernel, seq_len, starts),
        grid=(m // tm,),
        in_specs=[
            pl.BlockSpec((tm, win), lambda i: (i, wb)),
            pl.BlockSpec((SUBLANES, win), lambda i: (jnp.maximum(i * rb - 1, 0), wb)),
            pl.BlockSpec((conv_w.shape[0], fw), lambda i: (0, 0)),
            pl.BlockSpec((1, fw), lambda i: (0, 0)),
        ],
        out_specs=pl.BlockSpec((tm, fw), lambda i: (i, 0)),
        out_shape=jax.ShapeDtypeStruct((m, fw), F32),
        compiler_params=_params("parallel"),
        name="short_conv_mix",
    )(p, p, pad(conv_w), pad(out_gain))


def _rwkv_operands(p, prev_row, mu, wwa, g_up, w0, a0, k_k, k_a, n_decay):
    c = w0.shape[1]
    ps = p + (_shift_rows(p, prev_row, 1) - p) * mu
    r, k, v = ps[:, :c], ps[:, c:2 * c], ps[:, 2 * c:3 * c]
    lo = ps[:, 3 * c:3 * c + wwa.shape[0]]
    gd = ps[:, 3 * c:]
    lane = lax.broadcasted_iota(jnp.int32, lo.shape, 1)
    act = jnp.where(lane < n_decay, jnp.tanh(lo), lo)
    wa = _dot(act.astype(BF16), wwa)
    zw = w0 + wa[:, :c]
    w = -(jnp.maximum(-zw, 0.0) + jnp.log(1.0 + jnp.exp(-jnp.abs(zw)))) - 0.5
    iclr = 1.0 / (1.0 + jnp.exp(-(a0 + wa[:, c:])))
    g = _dot((1.0 / (1.0 + jnp.exp(-gd))).astype(BF16), g_up)
    kkf = k * k_k
    nrm = jnp.sqrt(_seg_sum(kkf * kkf, _group_ones(RWKV_HEAD)))
    kk = kkf / jnp.maximum(nrm, 1e-12)
    return r, k * (1.0 + (iclr - 1.0) * k_a), v, -jnp.exp(w), -kk, kk * iclr, g


def _wkv_kernel(n_decay, pr_ref, pk_ref, pv_ref, pl_ref, mu_ref, wwa_ref, gup_ref, w0_ref, a0_ref, kk_ref, ka_ref,
                lnw_ref, lnb_ref, rk_ref, o_ref, h_ref, prev_ref):
    @pl.when(pl.program_id(1) == 0)
    def _():
        h_ref[...] = jnp.zeros_like(h_ref)
        prev_ref[...] = jnp.zeros_like(prev_ref)

    TB, C = o_ref.shape
    L = RWKV_HEAD
    NCH = TB // L
    HPT = LANES // RWKV_HEAD
    NB = NCH * HPT
    p = jnp.concatenate([pr_ref[...], pk_ref[...], pv_ref[...], pl_ref[...]], axis=1)
    r, k, v, lw, a, b, g = _rwkv_operands(p, prev_ref[0:1, :], mu_ref[...], wwa_ref[...], gup_ref[...], w0_ref[...],
                                          a0_ref[...], kk_ref[...], ka_ref[...], n_decay)
    prev_ref[0:1, :] = p[TB - 1:TB, :]
    row = lax.broadcasted_iota(jnp.int32, (TB, TB), 0)
    col = lax.broadcasted_iota(jnp.int32, (TB, TB), 1)
    cum = _dot_hi(jnp.where((col <= row) & (row // L == col // L), 1.0, 0.0), lw)

    def chunk_rows(i):
        return jnp.concatenate([jnp.broadcast_to(cum[c * L + i:c * L + i + 1], (L, C)) for c in range(NCH)], axis=0)

    c_mid = chunk_rows(L // 2 - 1)
    c_end = chunk_rows(L - 1)
    e_fwd = jnp.exp(cum - c_mid)
    e_bwd = jnp.exp(c_mid - cum)
    e_end = jnp.exp(c_end - cum)
    bf = lambda z: z.astype(BF16)
    rt = bf(r * e_fwd)
    at = bf(a * jnp.exp(cum - lw - c_mid))
    bt = bf(b * e_bwd)
    kt = bf(k * e_bwd)
    bh = bf(b * e_end)
    kh = bf(k * e_end)
    vb = bf(v)

    lane = lax.broadcasted_iota(jnp.int32, (1, LANES), 1)
    hi = lax.broadcasted_iota(jnp.int32, (LANES, LANES), 0) // RWKV_HEAD
    hj = lax.broadcasted_iota(jnp.int32, (LANES, LANES), 1) // RWKV_HEAD
    same_head = hi == hj
    zero_b = jnp.zeros((), BF16)
    zeros_lt = jnp.zeros((L, LANES), BF16)

    tiles = range(C // LANES)
    chunks = range(NCH)
    tile_sl = [slice(t * LANES, (t + 1) * LANES) for t in tiles]
    rows = [slice(c * L, (c + 1) * L) for c in chunks]
    head_mask = [(lane // RWKV_HEAD) == h for h in range(HPT)]
    head_mask2 = [jnp.concatenate([m, m], axis=1) for m in head_mask]
    low_half = lane < RWKV_HEAD
    lr = lax.broadcasted_iota(jnp.int32, (L, LANES), 1) % L
    rr = lax.broadcasted_iota(jnp.int32, (L, LANES), 0)
    strict = lr < rr
    lower = lr <= rr
    lrp = lax.broadcasted_iota(jnp.int32, (L, NB * L), 1) % L
    rrp = lax.broadcasted_iota(jnp.int32, (L, NB * L), 0)
    blkp = lax.broadcasted_iota(jnp.int32, (L, NB * L), 1) // L
    same_blk = lambda size: (lrp // size) == (rrp // size)

    def block_diag(xp):
        return jnp.concatenate([jnp.where(blkp == q, xp, zero_b) for q in range(NB)], axis=0)

    def masked(x, t, c, h):
        return jnp.where(head_mask[h], x[rows[c], tile_sl[t]], zero_b)

    am = [[[masked(at, t, c, h) for h in range(HPT)] for c in chunks] for t in tiles]
    vm = [[[masked(vb, t, c, h) for h in range(HPT)] for c in chunks] for t in tiles]

    def score(t, c, h):
        b_, k_ = bt[rows[c], tile_sl[t]], kt[rows[c], tile_sl[t]]
        rhs = jnp.concatenate([b_, k_] if h == 0 else [k_, b_], axis=0)
        return _dot_nt(jnp.concatenate([am[t][c][h], masked(rt, t, c, h)], axis=0), rhs)

    s = [[[score(t, c, h) for h in range(HPT)] for c in chunks] for t in tiles]

    def pick(t, c, part, first):
        e, o = s[t][c][0][part * L:(part + 1) * L], s[t][c][1][part * L:(part + 1) * L]
        return jnp.where(low_half, e, o) if first == 0 else jnp.where(low_half, o, e)

    n_p = [jnp.concatenate([jnp.where(strict, pick(t, c, 0, 0), 0.0) for c in chunks], axis=1) for t in tiles]
    ak_p = [jnp.concatenate([bf(jnp.where(strict, pick(t, c, 0, 1), 0.0)) for c in chunks], axis=1) for t in tiles]
    rb = [[bf(jnp.where(lower, pick(t, c, 1, 0), 0.0)) for c in chunks] for t in tiles]
    rk = [[bf(jnp.where(lower, pick(t, c, 1, 1), 0.0)) for c in chunks] for t in tiles]
    eye_p = jnp.where(lrp == rrp, 1.0, 0.0)
    tinv = [eye_p + jnp.where(same_blk(2), n, 0.0) for n in n_p]
    size = 2
    while size < L:
        sub_diag = same_blk(2 * size) & jnp.logical_not(same_blk(size))
        e_bd = [block_diag(bf(jnp.where(sub_diag, n, 0.0))) for n in n_p]
        tb = [bf(ti) for ti in tinv]
        te = [bf(_dot(tb_, e_)) for tb_, e_ in zip(tb, e_bd)]
        tinv = [ti + _dot(te_, block_diag(tb_)) for ti, te_, tb_ in zip(tinv, te, tb)]
        size *= 2
    tb = [bf(ti) for ti in tinv]

    def place(x, c):
        return jnp.concatenate([x if cc == c else zeros_lt for cc in chunks], axis=1)

    w = [_dot(ak_p[t], jnp.concatenate([place(vm[t][c][h], c) for c in chunks for h in (1, 0)], axis=0))
         for t in tiles]

    def solve(t, c):
        wc = bf(w[t][:, c * LANES:(c + 1) * LANES])
        rhs = jnp.concatenate([jnp.concatenate([am[t][c][h], jnp.where(head_mask[h], wc, zero_b)], axis=1)
                               for h in range(HPT)], axis=0)
        return _dot(tb[t][:, c * LANES:(c + 1) * LANES], rhs)

    x = [[solve(t, c) for c in chunks] for t in tiles]

    def readout(t, c):
        xb = bf(x[t][c])
        rhs = jnp.concatenate([jnp.where(head_mask2[0], xb, zero_b), jnp.where(head_mask2[1], xb, zero_b),
                               jnp.concatenate([zeros_lt, vm[t][c][1]], axis=1),
                               jnp.concatenate([zeros_lt, vm[t][c][0]], axis=1)], axis=0)
        return _dot(jnp.concatenate([rb[t][c], rk[t][c]], axis=1), rhs)

    z = [[readout(t, c) for c in chunks] for t in tiles]

    h = [h_ref[t] for t in tiles]
    y_rows = [[None] * NCH for _ in tiles]
    for c in chunks:
        dec_mid = jnp.exp(cum[c * L + L // 2 - 1:c * L + L // 2, :])
        dec = jnp.exp(cum[c * L + L - 1:c * L + L, :])
        h_mid = [bf(h[t] * jnp.transpose(jnp.broadcast_to(dec_mid[:, tile_sl[t]], (LANES, LANES)))) for t in tiles]
        lhs = [bf(jnp.concatenate([x[t][c][:, :LANES],
                                   rt[rows[c], tile_sl[t]].astype(F32) + z[t][c][:, :LANES]], axis=0)) for t in tiles]
        ah = [_dot(lhs[t], h_mid[t]) for t in tiles]
        u = [ah[t][:L] + x[t][c][:, LANES:] for t in tiles]
        for t in tiles:
            y_rows[t][c] = ah[t][L:] + z[t][c][:, LANES:]
        upd = [_dot_tn(jnp.concatenate([bh[rows[c], tile_sl[t]], kh[rows[c], tile_sl[t]]], axis=0),
                       jnp.concatenate([bf(u[t]), vb[rows[c], tile_sl[t]]], axis=0)) for t in tiles]
        h = [h[t] * jnp.transpose(jnp.broadcast_to(dec[:, tile_sl[t]], (LANES, LANES)))
             + jnp.where(same_head, upd[t], 0.0) for t in tiles]
    for t in tiles:
        h_ref[t] = h[t]

    y = jnp.concatenate([jnp.concatenate(y_rows[t], axis=0) for t in tiles], axis=1)
    ones_bd = _group_ones(RWKV_HEAD)
    inv_n = 1.0 / RWKV_HEAD
    mean = _seg_sum(y, ones_bd) * inv_n
    d = y - mean
    var = _seg_sum(d * d, ones_bd) * inv_n
    yn = d * lax.rsqrt(var + LNX_EPS) * lnw_ref[...] + lnb_ref[...]
    bonus = _seg_sum(r * k * rk_ref[...], ones_bd) * v
    o_ref[...] = (yn + bonus) * g


def _rwkv_mix(p, col0, mu, wwa, g_up, w0, a0, k_k, k_a, ln_w, ln_b, r_k, batch, seq_len, n_decay):
    m = p.shape[0]
    nb = mu.shape[1]
    c = w0.shape[1]
    lw_ = g_up.shape[0]
    assert col0 % c == 0 and (col0 + 3 * c) % lw_ == 0 and nb == 3 * c + lw_
    cb, lb = col0 // c, (col0 + 3 * c) // lw_
    tb = min(WKV_STEP, seq_len)
    assert LANES == 2 * RWKV_HEAD and RWKV_HEAD & (RWKV_HEAD - 1) == 0, "kernel packs two power-of-two heads per lane tile"
    assert tb % RWKV_HEAD == 0 and seq_len % tb == 0 and c % LANES == 0
    nc = seq_len // tb
    vec = lambda n: pl.BlockSpec((1, n), lambda bi, ci: (0, 0))
    full = lambda a: pl.BlockSpec(a.shape, lambda bi, ci: (0, 0))
    return pl.pallas_call(
        functools.partial(_wkv_kernel, n_decay),
        grid=(batch, nc),
        in_specs=[pl.BlockSpec((tb, c), lambda bi, ci: (bi * nc + ci, cb)),
                  pl.BlockSpec((tb, c), lambda bi, ci: (bi * nc + ci, cb + 1)),
                  pl.BlockSpec((tb, c), lambda bi, ci: (bi * nc + ci, cb + 2)),
                  pl.BlockSpec((tb, lw_), lambda bi, ci: (bi * nc + ci, lb)),
                  vec(nb), full(wwa), full(g_up)] + [vec(c)] * 7,
        out_specs=pl.BlockSpec((tb, c), lambda bi, ci: (bi * nc + ci, 0)),
        out_shape=jax.ShapeDtypeStruct((m, c), F32),
        scratch_shapes=[pltpu.VMEM((c // LANES, LANES, LANES), F32), pltpu.VMEM((SUBLANES, nb), F32)],
        compiler_params=_params("parallel", "arbitrary"),
        name="rwkv7_mix",
    )(p, p, p, p, mu, wwa, g_up, w0, a0, k_k, k_a, ln_w, ln_b, r_k)


def _out_proj_kernel(phase, ya_ref, yb_ref, yc_ref, w_ref, g_ref, x_ref, o_ref):
    ra = ya_ref.shape[1]
    rc = ra + yb_ref.shape[1]
    y = (_dot(ya_ref[...].astype(BF16), w_ref[0:ra, :]) + _dot(yb_ref[...].astype(BF16), w_ref[ra:rc, :])
         + _dot(yc_ref[...].astype(BF16), w_ref[rc - phase:rc - phase + yc_ref.shape[1], :]))
    o_ref[...] = x_ref[...] + _rms(y, g_ref[...])


def _out_proj(ya, yb, yc, phase, w_all, layer, gain, x):
    m, d = x.shape
    tm = min(512, m)
    assert ya.shape[1] + yb.shape[1] - phase + yc.shape[1] <= w_all.shape[1]
    act = lambda a: pl.BlockSpec((tm, a.shape[1]), lambda i: (i, 0))
    return pl.pallas_call(
        functools.partial(_out_proj_kernel, phase),
        grid=(m // tm,),
        in_specs=[act(ya), act(yb), act(yc),
                  pl.BlockSpec((None,) + w_all.shape[1:], lambda i: (layer, 0, 0)),
                  pl.BlockSpec((1, d), lambda i: (0, 0)), act(x)],
        out_specs=act(x),
        out_shape=jax.ShapeDtypeStruct((m, d), F32),
        compiler_params=_params("parallel"),
        name="out_proj",
    )(ya, yb, yc, w_all, gain, x)


def _mlp_kernel(x_ref, gpre_ref, wu_ref, wd_ref, gpost_ref, o_ref, h_ref):
    j = pl.program_id(1)

    @pl.when(j == 0)
    def _():
        h_ref[...] = _rms(x_ref[...], gpre_ref[...]).astype(BF16)
        o_ref[...] = jnp.zeros_like(o_ref)

    f = jnp.maximum(_dot(h_ref[...], wu_ref[...].astype(BF16)), 0.0)
    o_ref[...] += _dot((f * f).astype(BF16), wd_ref[...].astype(BF16))

    @pl.when(j == pl.num_programs(1) - 1)
    def _():
        o_ref[...] = x_ref[...] + _rms(o_ref[...], gpost_ref[...])


def _mlp(x, g_pre, w_up_all, w_down_all, g_post, layer):
    m, d = x.shape
    ff = w_up_all.shape[2]
    tm = min(1024, m)
    tf = min(512, ff)
    return pl.pallas_call(
        _mlp_kernel,
        grid=(m // tm, ff // tf),
        in_specs=[
            pl.BlockSpec((tm, d), lambda i, j: (i, 0), pipeline_mode=pl.Buffered(1)),
            pl.BlockSpec((1, d), lambda i, j: (0, 0)),
            pl.BlockSpec((None, d, tf), lambda i, j: (layer, 0, j)),
            pl.BlockSpec((None, tf, d), lambda i, j: (layer, j, 0)),
            pl.BlockSpec((1, d), lambda i, j: (0, 0)),
        ],
        out_specs=pl.BlockSpec((tm, d), lambda i, j: (i, 0)),
        out_shape=jax.ShapeDtypeStruct((m, d), F32),
        scratch_shapes=[pltpu.VMEM((tm, d), BF16)],
        compiler_params=_params("parallel", "arbitrary"),
        name="relu2_mlp",
    )(x, g_pre, w_up_all, w_down_all, g_post)


def kernel(x, norm_mix_pre, norm_mix_post, norm_mlp_pre, norm_mlp_post, w_in, gm_v_gain, gm_ws, gm_bs,
           gm_out_gain, rk_mu, rk_w0, rk_w_up, rk_a0, rk_a_up, rk_g_up, rk_k_k, rk_k_a, rk_r_k, rk_ln_w,
           rk_ln_b, sc_conv, sc_out_gain, w_out, mlp_up, mlp_down):
    batch, seq_len, d = x.shape
    depth = w_in.shape[0]
    mix_a = gm_v_gain.shape[1]
    mix_b = rk_w0.shape[1]
    mix_c = sc_out_gain.shape[1]
    n_decay, n_iclr, n_gate = rk_w_up.shape[1], rk_a_up.shape[1], rk_g_up.shape[1]
    n_lora = n_decay + n_iclr
    lora_pad = -(-n_lora // LANES) * LANES
    col_b = 2 * mix_a
    col_lora = col_b + 3 * mix_b
    col_gate = col_lora + n_lora
    col_c = col_gate + n_gate
    row = lambda a: a.reshape(1, -1)

    xf = x.reshape(batch * seq_len, d)
    n_in = w_in.shape[2]
    n_proj = -(-n_in // PROJ_TILE_N) * PROJ_TILE_N
    w_proj = jnp.pad(w_in, ((0, 0), (0, 0), (0, n_proj - n_in))).astype(BF16)
    lora_blk = -(-(n_lora + n_gate) // LANES) * LANES
    conv_phase = col_c % LANES
    w_out_b = jnp.pad(w_out, ((0, 0), (0, LANES - conv_phase), (0, 0))).astype(BF16)
    for l in range(depth):
        mu_b = jnp.pad(rk_mu[l], (0, 3 * mix_b + lora_blk - rk_mu.shape[1])).reshape(1, -1)
        wwa = jnp.zeros((lora_pad, 2 * mix_b), F32)
        wwa = wwa.at[:n_decay, :mix_b].set(rk_w_up[l]).at[n_decay:n_lora, mix_b:].set(rk_a_up[l]).astype(BF16)
        g_up = jnp.zeros((lora_blk, mix_b), F32).at[n_lora:n_lora + n_gate].set(rk_g_up[l]).astype(BF16)

        p = _norm_matmul(xf, row(norm_mix_pre[l]), w_proj, l)

        ya = _gmlp(p, 0, row(gm_v_gain[l]), gm_ws[l], gm_bs[l].T, row(gm_out_gain[l]))
        yb = _rwkv_mix(p, col_b, mu_b, wwa, g_up, row(rk_w0[l]), row(rk_a0[l]), row(rk_k_k[l]),
                       row(rk_k_a[l]), row(rk_ln_w[l]), row(rk_ln_b[l]), row(rk_r_k[l]), batch, seq_len, n_decay)
        yc = _short_conv(p, col_c, sc_conv[l], row(sc_out_gain[l]), seq_len)

        xf = _out_proj(ya, yb, yc, conv_phase, w_out_b, l, row(norm_mix_post[l]), xf)
        xf = _mlp(xf, row(norm_mlp_pre[l]), mlp_up, mlp_down, row(norm_mlp_post[l]), l)
    return xf.reshape(batch, seq_len, d)
```

```python
import functools

import jax
import jax.numpy as jnp
from jax import lax
from jax.experimental import pallas as pl
from jax.experimental.pallas import tpu as pltpu

F32 = jnp.float32
BF16 = jnp.bfloat16
HIGHEST = lax.Precision.HIGHEST

RMS_EPS = 1e-6
LNX_EPS = 64e-5
RWKV_HEAD = 64
CONV_GROUP_W = 64
LANES = 128
SUBLANES = 8
WKV_STEP = 128
PROJ_TILE_N = 1024
WKV_TILE_GROUP = 8
V7X_VMEM_BYTES = 64 * 1024 * 1024
VMEM_LIMIT = 56 * 1024 * 1024
VMEM_LIMIT_MLP = V7X_VMEM_BYTES - 2 * 1024 * 1024


def _params(*sem, vmem=VMEM_LIMIT):
    return pltpu.CompilerParams(dimension_semantics=sem, vmem_limit_bytes=vmem)


def _rms(x, gain):
    return x * lax.rsqrt(jnp.mean(x * x, axis=-1, keepdims=True) + RMS_EPS) * gain


def _dot(a, b):
    return jnp.dot(a, b, preferred_element_type=F32)


def _dot_hi(a, b):
    return jnp.dot(a, b, preferred_element_type=F32, precision=HIGHEST)


def _dot_nt(a, b):
    return lax.dot_general(a, b, (((1,), (1,)), ((), ())), preferred_element_type=F32)


def _dot_tn(a, b):
    return lax.dot_general(a, b, (((0,), (0,)), ((), ())), preferred_element_type=F32)


def _group_ones(width):
    i = lax.broadcasted_iota(jnp.int32, (LANES, LANES), 0) // width
    j = lax.broadcasted_iota(jnp.int32, (LANES, LANES), 1) // width
    return jnp.where(i == j, 1.0, 0.0).astype(BF16)


def _seg_sum(x, ones_bd):
    hi = x.astype(BF16)
    lo = (x - hi.astype(F32)).astype(BF16)
    cols = []
    for c in range(x.shape[-1] // LANES):
        sl = slice(c * LANES, (c + 1) * LANES)
        cols.append(_dot(hi[:, sl], ones_bd) + _dot(lo[:, sl], ones_bd))
    return cols[0] if len(cols) == 1 else jnp.concatenate(cols, axis=-1)


def _shift_rows(x, prev_rows, n):
    rolled = pltpu.roll(x, n, axis=0)
    row = lax.broadcasted_iota(jnp.int32, x.shape, 0)
    p = prev_rows.shape[0]
    for j in range(n):
        rolled = jnp.where(row == j, prev_rows[p - n + j:p - n + j + 1, :], rolled)
    return rolled


def _norm_matmul_kernel(x_ref, g_ref, w_ref, o_ref, h_ref):
    @pl.when(pl.program_id(1) == 0)
    def _():
        h_ref[...] = _rms(x_ref[...], g_ref[...]).astype(BF16)

    o_ref[...] = _dot(h_ref[...], w_ref[...])


def _norm_matmul(x, gain, w_all, layer):
    m, d = x.shape
    n = w_all.shape[2]
    tm = min(1024, m)
    tn = PROJ_TILE_N
    return pl.pallas_call(
        _norm_matmul_kernel,
        grid=(m // tm, n // tn),
        in_specs=[
            pl.BlockSpec((tm, d), lambda i, j: (i, 0)),
            pl.BlockSpec((1, d), lambda i, j: (0, 0)),
            pl.BlockSpec((None, d, tn), lambda i, j: (layer, 0, j)),
        ],
        out_specs=pl.BlockSpec((tm, tn), lambda i, j: (i, j)),
        out_shape=jax.ShapeDtypeStruct((m, n), F32),
        scratch_shapes=[pltpu.VMEM((tm, d), BF16)],
        compiler_params=_params("parallel", "arbitrary"),
        name="in_proj",
    )(x, gain, w_all)


def _gelu_tanh(x):
    return 0.5 * x * (1.0 + jnp.tanh(0.7978845608028654 * (x + 0.044715 * (x * x * x))))


def _gmlp_kernel(pu_ref, pv_ref, vg_ref, ws_ref, bst_ref, og_ref, o_ref):
    groups, chunk, _ = ws_ref.shape
    tm = pu_ref.shape[0]
    zu = _gelu_tanh(pu_ref[...])
    zv = _gelu_tanh(pv_ref[...])
    row = lax.broadcasted_iota(jnp.int32, (chunk, chunk), 0)
    col = lax.broadcasted_iota(jnp.int32, (chunk, chunk), 1)
    bst = bst_ref[...]
    for g in range(groups):
        sl = slice(g * LANES, (g + 1) * LANES)
        u = zu[:, sl]
        v = _rms(zv[:, sl], vg_ref[:, sl]).astype(BF16)
        w = jnp.where(col <= row, ws_ref[g], 0.0).astype(BF16)
        bias = bst[:, g:g + 1]
        for c in range(tm // chunk):
            rows = slice(c * chunk, (c + 1) * chunk)
            mixed = _dot(w, v[rows]) + bias
            o_ref[rows, sl] = _rms(u[rows] * mixed, og_ref[:, sl])


def _gmlp(p, col0, v_gain, ws, bs_t, out_gain):
    m = p.shape[0]
    groups, chunk, _ = ws.shape
    half = v_gain.shape[1]
    cb = col0 // half
    assert cb * half == col0
    tm = min(512, m)
    return pl.pallas_call(
        _gmlp_kernel,
        grid=(m // tm,),
        in_specs=[
            pl.BlockSpec((tm, half), lambda i: (i, cb)),
            pl.BlockSpec((tm, half), lambda i: (i, cb + 1)),
            pl.BlockSpec((1, half), lambda i: (0, 0)),
            pl.BlockSpec((groups, chunk, chunk), lambda i: (0, 0, 0)),
            pl.BlockSpec((chunk, groups), lambda i: (0, 0)),
            pl.BlockSpec((1, half), lambda i: (0, 0)),
        ],
        out_specs=pl.BlockSpec((tm, half), lambda i: (i, 0)),
        out_shape=jax.ShapeDtypeStruct((m, half), F32),
        compiler_params=_params("parallel"),
        name="gmlp_mix",
    )(p, p, v_gain, ws, bs_t, out_gain)


def _conv_kernel(seq_len, starts, p_ref, prev_ref, cw_ref, og_ref, o_ref):
    tm = p_ref.shape[0]
    fw = o_ref.shape[1]
    sg, sc, sh = starts
    p = p_ref[...]
    gb, z = p[:, sg:sg + fw], p[:, sc:sc + fw] * p[:, sh:sh + fw]
    pv = prev_ref[...]
    seq_start = (pl.program_id(0) * tm) % seq_len == 0
    zp = jnp.where(seq_start, 0.0, pv[:, sc:sc + fw] * pv[:, sh:sh + fw])
    cw = cw_ref[...]
    taps = cw.shape[0]
    zc = cw[taps - 1:taps, :] * z
    for s in range(1, taps):
        zc = zc + cw[taps - 1 - s:taps - s, :] * _shift_rows(z, zp, s)
    y = gb * zc
    ms = _seg_sum(y * y, _group_ones(CONV_GROUP_W)) * (1.0 / CONV_GROUP_W)
    o_ref[...] = y * lax.rsqrt(ms + RMS_EPS) * og_ref[...]


def _short_conv(p, col0, conv_w, out_gain, seq_len):
    m, n_all = p.shape
    c = out_gain.shape[1]
    phase = col0 % LANES
    fw = c + LANES
    win = 4 * c
    wb = col0 // win
    assert (wb + 1) * win <= n_all and c % LANES == 0 and phase % CONV_GROUP_W == 0
    starts = tuple(col0 + j * c - phase - wb * win for j in range(3))
    assert starts[0] >= 0 and starts[2] + fw <= win
    pad = lambda a: jnp.pad(a, ((0, 0), (phase, LANES - phase)))
    tm = min(512, seq_len)
    assert seq_len % tm == 0, "a row block must not straddle two sequences"
    rb = tm // SUBLANES
    return pl.pallas_call(
        functools.partial(_conv_kernel, seq_len, starts),
        grid=(m // tm,),
        in_specs=[
            pl.BlockSpec((tm, win), lambda i: (i, wb)),
            pl.BlockSpec((SUBLANES, win), lambda i: (jnp.maximum(i * rb - 1, 0), wb)),
            pl.BlockSpec((conv_w.shape[0], fw), lambda i: (0, 0)),
            pl.BlockSpec((1, fw), lambda i: (0, 0)),
        ],
        out_specs=pl.BlockSpec((tm, fw), lambda i: (i, 0)),
        out_shape=jax.ShapeDtypeStruct((m, fw), F32),
        compiler_params=_params("parallel"),
        name="short_conv_mix",
    )(p, p, pad(conv_w), pad(out_gain))


def _rwkv_operands(p, prev_row, mu, wwa, g_up, w0, a0, k_k, k_a, n_decay):
    c = w0.shape[1]
    ps = p + (_shift_rows(p, prev_row, 1) - p) * mu
    r, k, v = ps[:, :c], ps[:, c:2 * c], ps[:, 2 * c:3 * c]
    lo = ps[:, 3 * c:3 * c + wwa.shape[0]]
    gd = ps[:, 3 * c:]
    lane = lax.broadcasted_iota(jnp.int32, lo.shape, 1)
    act = jnp.where(lane < n_decay, jnp.tanh(lo), lo)
    wa = _dot(act.astype(BF16), wwa)
    zw = w0 + wa[:, :c]
    w = -(jnp.maximum(-zw, 0.0) + jnp.log(1.0 + jnp.exp(-jnp.abs(zw)))) - 0.5
    iclr = 1.0 / (1.0 + jnp.exp(-(a0 + wa[:, c:])))
    g = _dot((1.0 / (1.0 + jnp.exp(-gd))).astype(BF16), g_up)
    kkf = k * k_k
    nrm = jnp.sqrt(_seg_sum(kkf * kkf, _group_ones(RWKV_HEAD)))
    kk = kkf / jnp.maximum(nrm, 1e-12)
    return r, k * (1.0 + (iclr - 1.0) * k_a), v, -jnp.exp(w), -kk, kk * iclr, g


def _wkv_kernel(n_decay, pr_ref, pk_ref, pv_ref, pl_ref, mu_ref, wwa_ref, gup_ref, w0_ref, a0_ref, kk_ref, ka_ref,
                lnw_ref, lnb_ref, rk_ref, o_ref, h_ref, prev_ref):
    @pl.when(pl.program_id(1) == 0)
    def _():
        h_ref[...] = jnp.zeros_like(h_ref)
        prev_ref[...] = jnp.zeros_like(prev_ref)

    TB, C = o_ref.shape
    L = RWKV_HEAD
    NCH = TB // L
    HPT = LANES // RWKV_HEAD
    NB = NCH * HPT
    p = jnp.concatenate([pr_ref[...], pk_ref[...], pv_ref[...], pl_ref[...]], axis=1)
    r, k, v, lw, a, b, g = _rwkv_operands(p, prev_ref[0:1, :], mu_ref[...], wwa_ref[...], gup_ref[...], w0_ref[...],
                                          a0_ref[...], kk_ref[...], ka_ref[...], n_decay)
    prev_ref[0:1, :] = p[TB - 1:TB, :]
    row = lax.broadcasted_iota(jnp.int32, (TB, TB), 0)
    col = lax.broadcasted_iota(jnp.int32, (TB, TB), 1)
    cum = _dot_hi(jnp.where((col <= row) & (row // L == col // L), 1.0, 0.0), lw)

    def chunk_rows(i):
        return jnp.concatenate([jnp.broadcast_to(cum[c * L + i:c * L + i + 1], (L, C)) for c in range(NCH)], axis=0)

    c_mid = chunk_rows(L // 2 - 1)
    c_end = chunk_rows(L - 1)
    e_fwd = jnp.exp(cum - c_mid)
    e_bwd = jnp.exp(c_mid - cum)
    e_end = jnp.exp(c_end - cum)
    bf = lambda z: z.astype(BF16)
    rt = bf(r * e_fwd)
    at = bf(a * jnp.exp(cum - lw - c_mid))
    bt = bf(b * e_bwd)
    kt = bf(k * e_bwd)
    bh = bf(b * e_end)
    kh = bf(k * e_end)
    vb = bf(v)

    lane = lax.broadcasted_iota(jnp.int32, (1, LANES), 1)
    hi = lax.broadcasted_iota(jnp.int32, (LANES, LANES), 0) // RWKV_HEAD
    hj = lax.broadcasted_iota(jnp.int32, (LANES, LANES), 1) // RWKV_HEAD
    same_head = hi == hj
    zero_b = jnp.zeros((), BF16)
    zeros_lt = jnp.zeros((L, LANES), BF16)

    chunks = range(NCH)
    rows = [slice(c * L, (c + 1) * L) for c in chunks]
    head_mask = [(lane // RWKV_HEAD) == h for h in range(HPT)]
    head_mask2 = [jnp.concatenate([m, m], axis=1) for m in head_mask]
    low_half = lane < RWKV_HEAD
    lr = lax.broadcasted_iota(jnp.int32, (L, LANES), 1) % L
    rr = lax.broadcasted_iota(jnp.int32, (L, LANES), 0)
    strict = lr < rr
    lower = lr <= rr
    lrp = lax.broadcasted_iota(jnp.int32, (L, NB * L), 1) % L
    rrp = lax.broadcasted_iota(jnp.int32, (L, NB * L), 0)
    blkp = lax.broadcasted_iota(jnp.int32, (L, NB * L), 1) // L
    same_blk = lambda size: (lrp // size) == (rrp // size)

    def block_diag(xp):
        return jnp.concatenate([jnp.where(blkp == q, xp, zero_b) for q in range(NB)], axis=0)

    def run_tiles(t0, nt):
        tiles = range(nt)
        tile_sl = [slice((t0 + t) * LANES, (t0 + t + 1) * LANES) for t in tiles]

        def masked(x, t, c, h):
            return jnp.where(head_mask[h], x[rows[c], tile_sl[t]], zero_b)

        am = [[[masked(at, t, c, h) for h in range(HPT)] for c in chunks] for t in tiles]
        vm = [[[masked(vb, t, c, h) for h in range(HPT)] for c in chunks] for t in tiles]

        def score(t, c, h):
            b_, k_ = bt[rows[c], tile_sl[t]], kt[rows[c], tile_sl[t]]
            rhs = jnp.concatenate([b_, k_] if h == 0 else [k_, b_], axis=0)
            return _dot_nt(jnp.concatenate([am[t][c][h], masked(rt, t, c, h)], axis=0), rhs)

        s = [[[score(t, c, h) for h in range(HPT)] for c in chunks] for t in tiles]

        def pick(t, c, part, first):
            e, o = s[t][c][0][part * L:(part + 1) * L], s[t][c][1][part * L:(part + 1) * L]
            return jnp.where(low_half, e, o) if first == 0 else jnp.where(low_half, o, e)

        n_p = [jnp.concatenate([jnp.where(strict, pick(t, c, 0, 0), 0.0) for c in chunks], axis=1) for t in tiles]
        ak_p = [jnp.concatenate([bf(jnp.where(strict, pick(t, c, 0, 1), 0.0)) for c in chunks], axis=1) for t in tiles]
        rb = [[bf(jnp.where(lower, pick(t, c, 1, 0), 0.0)) for c in chunks] for t in tiles]
        rk = [[bf(jnp.where(lower, pick(t, c, 1, 1), 0.0)) for c in chunks] for t in tiles]
        eye_p = jnp.where(lrp == rrp, 1.0, 0.0)
        tinv = [eye_p + jnp.where(same_blk(2), n, 0.0) for n in n_p]
        size = 2
        while size < L:
            sub_diag = same_blk(2 * size) & jnp.logical_not(same_blk(size))
            e_bd = [block_diag(bf(jnp.where(sub_diag, n, 0.0))) for n in n_p]
            tb = [bf(ti) for ti in tinv]
            te = [bf(_dot(tb_, e_)) for tb_, e_ in zip(tb, e_bd)]
            tinv = [ti + _dot(te_, block_diag(tb_)) for ti, te_, tb_ in zip(tinv, te, tb)]
            size *= 2
        tb = [bf(ti) for ti in tinv]

        def place(x, c):
            return jnp.concatenate([x if cc == c else zeros_lt for cc in chunks], axis=1)

        w = [_dot(ak_p[t], jnp.concatenate([place(vm[t][c][h], c) for c in chunks for h in (1, 0)], axis=0))
             for t in tiles]

        def solve(t, c):
            wc = bf(w[t][:, c * LANES:(c + 1) * LANES])
            rhs = jnp.concatenate([jnp.concatenate([am[t][c][h], jnp.where(head_mask[h], wc, zero_b)], axis=1)
                                   for h in range(HPT)], axis=0)
            return _dot(tb[t][:, c * LANES:(c + 1) * LANES], rhs)

        x = [[solve(t, c) for c in chunks] for t in tiles]

        def readout(t, c):
            xb = bf(x[t][c])
            rhs = jnp.concatenate([jnp.where(head_mask2[0], xb, zero_b), jnp.where(head_mask2[1], xb, zero_b),
                                   jnp.concatenate([zeros_lt, vm[t][c][1]], axis=1),
                                   jnp.concatenate([zeros_lt, vm[t][c][0]], axis=1)], axis=0)
            return _dot(jnp.concatenate([rb[t][c], rk[t][c]], axis=1), rhs)

        z = [[readout(t, c) for c in chunks] for t in tiles]

        h = [h_ref[t0 + t] for t in tiles]
        y_rows = [[None] * NCH for _ in tiles]
        for c in chunks:
            dec_mid = jnp.exp(cum[c * L + L // 2 - 1:c * L + L // 2, :])
            dec = jnp.exp(cum[c * L + L - 1:c * L + L, :])
            h_mid = [bf(h[t] * jnp.transpose(jnp.broadcast_to(dec_mid[:, tile_sl[t]], (LANES, LANES)))) for t in tiles]
            lhs = [bf(jnp.concatenate([x[t][c][:, :LANES],
                                       rt[rows[c], tile_sl[t]].astype(F32) + z[t][c][:, :LANES]], axis=0)) for t in tiles]
            ah = [_dot(lhs[t], h_mid[t]) for t in tiles]
            u = [ah[t][:L] + x[t][c][:, LANES:] for t in tiles]
            for t in tiles:
                y_rows[t][c] = ah[t][L:] + z[t][c][:, LANES:]
            upd = [_dot_tn(jnp.concatenate([bh[rows[c], tile_sl[t]], kh[rows[c], tile_sl[t]]], axis=0),
                           jnp.concatenate([bf(u[t]), vb[rows[c], tile_sl[t]]], axis=0)) for t in tiles]
            h = [h[t] * jnp.transpose(jnp.broadcast_to(dec[:, tile_sl[t]], (LANES, LANES)))
                 + jnp.where(same_head, upd[t], 0.0) for t in tiles]
        for t in tiles:
            h_ref[t0 + t] = h[t]
        return [jnp.concatenate(y_rows[t], axis=0) for t in tiles]

    n_tiles = C // LANES
    y_tiles = []
    for t0 in range(0, n_tiles, WKV_TILE_GROUP):
        y_tiles += run_tiles(t0, min(WKV_TILE_GROUP, n_tiles - t0))
    y = jnp.concatenate(y_tiles, axis=1)
    ones_bd = _group_ones(RWKV_HEAD)
    inv_n = 1.0 / RWKV_HEAD
    mean = _seg_sum(y, ones_bd) * inv_n
    d = y - mean
    var = _seg_sum(d * d, ones_bd) * inv_n
    yn = d * lax.rsqrt(var + LNX_EPS) * lnw_ref[...] + lnb_ref[...]
    bonus = _seg_sum(r * k * rk_ref[...], ones_bd) * v
    o_ref[...] = (yn + bonus) * g


def _rwkv_mix(p, col0, mu, wwa, g_up, w0, a0, k_k, k_a, ln_w, ln_b, r_k, batch, seq_len, n_decay):
    m = p.shape[0]
    nb = mu.shape[1]
    c = w0.shape[1]
    lw_ = g_up.shape[0]
    assert col0 % c == 0 and (col0 + 3 * c) % lw_ == 0 and nb == 3 * c + lw_
    cb, lb = col0 // c, (col0 + 3 * c) // lw_
    tb = min(WKV_STEP, seq_len)
    assert LANES == 2 * RWKV_HEAD and RWKV_HEAD & (RWKV_HEAD - 1) == 0, "kernel packs two power-of-two heads per lane tile"
    assert tb % RWKV_HEAD == 0 and seq_len % tb == 0 and c % LANES == 0
    nc = seq_len // tb
    vec = lambda n: pl.BlockSpec((1, n), lambda bi, ci: (0, 0))
    full = lambda a: pl.BlockSpec(a.shape, lambda bi, ci: (0, 0))
    return pl.pallas_call(
        functools.partial(_wkv_kernel, n_decay),
        grid=(batch, nc),
        in_specs=[pl.BlockSpec((tb, c), lambda bi, ci: (bi * nc + ci, cb)),
                  pl.BlockSpec((tb, c), lambda bi, ci: (bi * nc + ci, cb + 1)),
                  pl.BlockSpec((tb, c), lambda bi, ci: (bi * nc + ci, cb + 2)),
                  pl.BlockSpec((tb, lw_), lambda bi, ci: (bi * nc + ci, lb)),
                  vec(nb), full(wwa), full(g_up)] + [vec(c)] * 7,
        out_specs=pl.BlockSpec((tb, c), lambda bi, ci: (bi * nc + ci, 0)),
        out_shape=jax.ShapeDtypeStruct((m, c), F32),
        scratch_shapes=[pltpu.VMEM((c // LANES, LANES, LANES), F32), pltpu.VMEM((SUBLANES, nb), F32)],
        compiler_params=_params("parallel", "arbitrary"),
        name="rwkv7_mix",
    )(p, p, p, p, mu, wwa, g_up, w0, a0, k_k, k_a, ln_w, ln_b, r_k)


def _out_proj_kernel(phase, ya_ref, yb_ref, yc_ref, w_ref, g_ref, x_ref, o_ref, wb_ref):
    @pl.when(pl.program_id(0) == 0)
    def _():
        wb_ref[...] = w_ref[...].astype(BF16)

    ra = ya_ref.shape[1]
    rc = ra + yb_ref.shape[1]
    rows, d = wb_ref.shape
    start = rc - phase
    full = (rows - start) // LANES * LANES
    yc = yc_ref[...].astype(BF16)
    w_tail = jnp.concatenate([wb_ref[start + full:rows, :],
                              jnp.zeros((LANES - (rows - start - full), d), BF16)], axis=0)
    y = (_dot(ya_ref[...].astype(BF16), wb_ref[0:ra, :]) + _dot(yb_ref[...].astype(BF16), wb_ref[ra:rc, :])
         + _dot(yc[:, :full], wb_ref[start:start + full, :]) + _dot(yc[:, full:], w_tail))
    o_ref[...] = x_ref[...] + _rms(y, g_ref[...])


def _out_proj(ya, yb, yc, phase, w_all, layer, gain, x):
    m, d = x.shape
    rows = w_all.shape[1]
    tm = min(256, m)
    start = ya.shape[1] + yb.shape[1] - phase
    assert yc.shape[1] == (rows - start) // LANES * LANES + LANES and 0 < (rows - start) % LANES
    act = lambda a: pl.BlockSpec((tm, a.shape[1]), lambda i: (i, 0))
    return pl.pallas_call(
        functools.partial(_out_proj_kernel, phase),
        grid=(m // tm,),
        in_specs=[act(ya), act(yb), act(yc),
                  pl.BlockSpec((None, rows, d), lambda i: (layer, 0, 0), pipeline_mode=pl.Buffered(1)),
                  pl.BlockSpec((1, d), lambda i: (0, 0)), act(x)],
        out_specs=act(x),
        out_shape=jax.ShapeDtypeStruct((m, d), F32),
        scratch_shapes=[pltpu.VMEM((rows, d), BF16)],
        compiler_params=_params("arbitrary"),
        name="out_proj",
    )(ya, yb, yc, w_all, gain, x)


def _mlp_kernel(x_ref, gpre_ref, wu_ref, wd_ref, gpost_ref, o_ref, h_ref):
    j = pl.program_id(1)

    @pl.when(j == 0)
    def _():
        h_ref[...] = _rms(x_ref[...], gpre_ref[...]).astype(BF16)
        o_ref[...] = jnp.zeros_like(o_ref)

    f = jnp.maximum(_dot(h_ref[...], wu_ref[...].astype(BF16)), 0.0)
    o_ref[...] += _dot((f * f).astype(BF16), wd_ref[...].astype(BF16))

    @pl.when(j == pl.num_programs(1) - 1)
    def _():
        o_ref[...] = x_ref[...] + _rms(o_ref[...], gpost_ref[...])


def _mlp(x, g_pre, w_up_all, w_down_all, g_post, layer):
    m, d = x.shape
    ff = w_up_all.shape[2]
    tm = min(1024, m)
    tf = min(512, ff)
    return pl.pallas_call(
        _mlp_kernel,
        grid=(m // tm, ff // tf),
        in_specs=[
            pl.BlockSpec((tm, d), lambda i, j: (i, 0)),
            pl.BlockSpec((1, d), lambda i, j: (0, 0)),
            pl.BlockSpec((None, d, tf), lambda i, j: (layer, 0, j)),
            pl.BlockSpec((None, tf, d), lambda i, j: (layer, j, 0)),
            pl.BlockSpec((1, d), lambda i, j: (0, 0)),
        ],
        out_specs=pl.BlockSpec((tm, d), lambda i, j: (i, 0)),
        out_shape=jax.ShapeDtypeStruct((m, d), F32),
        scratch_shapes=[pltpu.VMEM((tm, d), BF16)],
        compiler_params=_params("parallel", "arbitrary", vmem=VMEM_LIMIT_MLP),
        name="relu2_mlp",
    )(x, g_pre, w_up_all, w_down_all, g_post)


def kernel(x, norm_mix_pre, norm_mix_post, norm_mlp_pre, norm_mlp_post, w_in, gm_v_gain, gm_ws, gm_bs,
           gm_out_gain, rk_mu, rk_w0, rk_w_up, rk_a0, rk_a_up, rk_g_up, rk_k_k, rk_k_a, rk_r_k, rk_ln_w,
           rk_ln_b, sc_conv, sc_out_gain, w_out, mlp_up, mlp_down):
    batch, seq_len, d = x.shape
    depth = w_in.shape[0]
    mix_a = gm_v_gain.shape[1]
    mix_b = rk_w0.shape[1]
    mix_c = sc_out_gain.shape[1]
    n_decay, n_iclr, n_gate = rk_w_up.shape[1], rk_a_up.shape[1], rk_g_up.shape[1]
    n_lora = n_decay + n_iclr
    lora_pad = -(-n_lora // LANES) * LANES
    col_b = 2 * mix_a
    col_lora = col_b + 3 * mix_b
    col_gate = col_lora + n_lora
    col_c = col_gate + n_gate
    row = lambda a: a.reshape(1, -1)

    xf = x.reshape(batch * seq_len, d)
    n_in = w_in.shape[2]
    n_proj = -(-n_in // PROJ_TILE_N) * PROJ_TILE_N
    w_proj = jnp.concatenate([w_in.astype(BF16), jnp.zeros((depth, d, n_proj - n_in), BF16)], axis=2)
    lora_blk = -(-(n_lora + n_gate) // LANES) * LANES
    conv_phase = col_c % LANES
    for l in range(depth):
        mu_b = jnp.pad(rk_mu[l], (0, 3 * mix_b + lora_blk - rk_mu.shape[1])).reshape(1, -1)
        wwa = jnp.zeros((lora_pad, 2 * mix_b), F32)
        wwa = wwa.at[:n_decay, :mix_b].set(rk_w_up[l]).at[n_decay:n_lora, mix_b:].set(rk_a_up[l]).astype(BF16)
        g_up = jnp.zeros((lora_blk, mix_b), F32).at[n_lora:n_lora + n_gate].set(rk_g_up[l]).astype(BF16)

        p = _norm_matmul(xf, row(norm_mix_pre[l]), w_proj, l)

        ya = _gmlp(p, 0, row(gm_v_gain[l]), gm_ws[l], gm_bs[l].T, row(gm_out_gain[l]))
        yb = _rwkv_mix(p, col_b, mu_b, wwa, g_up, row(rk_w0[l]), row(rk_a0[l]), row(rk_k_k[l]),
                       row(rk_k_a[l]), row(rk_ln_w[l]), row(rk_ln_b[l]), row(rk_r_k[l]), batch, seq_len, n_decay)
        yc = _short_conv(p, col_c, sc_conv[l], row(sc_out_gain[l]), seq_len)

        xf = _out_proj(ya, yb, yc, conv_phase, w_out, l, row(norm_mix_post[l]), xf)
        xf = _mlp(xf, row(norm_mlp_pre[l]), mlp_up, mlp_down, row(norm_mlp_post[l]), l)
    return xf.reshape(batch, seq_len, d)
```

```python
import functools

import jax
import jax.numpy as jnp
from jax import lax
from jax.experimental import pallas as pl
from jax.experimental.pallas import tpu as pltpu

F32 = jnp.float32
BF16 = jnp.bfloat16
HIGHEST = lax.Precision.HIGHEST

RMS_EPS = 1e-6
LNX_EPS = 64e-5
RWKV_HEAD = 64
CONV_GROUP_W = 64
LANES = 128
SUBLANES = 8
WKV_STEP = 256
PROJ_TILE_N = 1024
WKV_TILE_GROUP = 8
V7X_VMEM_BYTES = 64 * 1024 * 1024
VMEM_LIMIT = 56 * 1024 * 1024
VMEM_LIMIT_MLP = V7X_VMEM_BYTES - 2 * 1024 * 1024


def _params(*sem, vmem=VMEM_LIMIT):
    return pltpu.CompilerParams(dimension_semantics=sem, vmem_limit_bytes=vmem)


def _rms(x, gain):
    return x * lax.rsqrt(jnp.mean(x * x, axis=-1, keepdims=True) + RMS_EPS) * gain


def _dot(a, b):
    return jnp.dot(a, b, preferred_element_type=F32)


def _dot_exact_lhs(a, b):
    b0 = b.astype(BF16)
    r1 = b - b0.astype(F32)
    b1 = r1.astype(BF16)
    b2 = (r1 - b1.astype(F32)).astype(BF16)
    ab = a.astype(BF16)
    return _dot(jnp.concatenate([ab, ab, ab], axis=1), jnp.concatenate([b0, b1, b2], axis=0))


def _dot_nt(a, b):
    return lax.dot_general(a, b, (((1,), (1,)), ((), ())), preferred_element_type=F32)


def _dot_tn(a, b):
    return lax.dot_general(a, b, (((0,), (0,)), ((), ())), preferred_element_type=F32)


def _group_ones(width):
    i = (lax.broadcasted_iota(jnp.int32, (2 * LANES, LANES), 0) % LANES) // width
    j = lax.broadcasted_iota(jnp.int32, (2 * LANES, LANES), 1) // width
    return jnp.where(i == j, 1.0, 0.0).astype(BF16)


def _seg_sum(x, ones2):
    hi = x.astype(BF16)
    lo = (x - hi.astype(F32)).astype(BF16)
    cols = []
    for c in range(x.shape[-1] // LANES):
        sl = slice(c * LANES, (c + 1) * LANES)
        cols.append(_dot(jnp.concatenate([hi[:, sl], lo[:, sl]], axis=1), ones2))
    return cols[0] if len(cols) == 1 else jnp.concatenate(cols, axis=-1)


def _shift_rows(x, prev_rows, n):
    rolled = pltpu.roll(x, n, axis=0)
    row = lax.broadcasted_iota(jnp.int32, x.shape, 0)
    p = prev_rows.shape[0]
    for j in range(n):
        rolled = jnp.where(row == j, prev_rows[p - n + j:p - n + j + 1, :], rolled)
    return rolled


def _norm_matmul_kernel(x_ref, g_ref, w_ref, o_ref, h_ref):
    @pl.when(pl.program_id(1) == 0)
    def _():
        h_ref[...] = _rms(x_ref[...], g_ref[...]).astype(BF16)

    o_ref[...] = _dot(h_ref[...], w_ref[...])


def _norm_matmul(x, gain, w_all, layer):
    m, d = x.shape
    n = w_all.shape[2]
    tm = min(1024, m)
    tn = PROJ_TILE_N
    return pl.pallas_call(
        _norm_matmul_kernel,
        grid=(m // tm, n // tn),
        in_specs=[
            pl.BlockSpec((tm, d), lambda i, j: (i, 0)),
            pl.BlockSpec((1, d), lambda i, j: (0, 0)),
            pl.BlockSpec((None, d, tn), lambda i, j: (layer, 0, j)),
        ],
        out_specs=pl.BlockSpec((tm, tn), lambda i, j: (i, j)),
        out_shape=jax.ShapeDtypeStruct((m, n), F32),
        scratch_shapes=[pltpu.VMEM((tm, d), BF16)],
        compiler_params=_params("parallel", "arbitrary"),
        name="in_proj",
    )(x, gain, w_all)


def _gelu_tanh(x):
    return 0.5 * x * (1.0 + jnp.tanh(0.7978845608028654 * (x + 0.044715 * (x * x * x))))


def _gmlp_kernel(pu_ref, pv_ref, vg_ref, ws_ref, bst_ref, og_ref, o_ref):
    groups, chunk, _ = ws_ref.shape
    tm = pu_ref.shape[0]
    zu = _gelu_tanh(pu_ref[...])
    zv = _gelu_tanh(pv_ref[...])
    row = lax.broadcasted_iota(jnp.int32, (chunk, chunk), 0)
    col = lax.broadcasted_iota(jnp.int32, (chunk, chunk), 1)
    bst = bst_ref[...]
    for g in range(groups):
        sl = slice(g * LANES, (g + 1) * LANES)
        u = zu[:, sl]
        v = _rms(zv[:, sl], vg_ref[:, sl]).astype(BF16)
        w = jnp.where(col <= row, ws_ref[g], 0.0).astype(BF16)
        bias = bst[:, g:g + 1]
        for c in range(tm // chunk):
            rows = slice(c * chunk, (c + 1) * chunk)
            mixed = _dot(w, v[rows]) + bias
            o_ref[rows, sl] = _rms(u[rows] * mixed, og_ref[:, sl])


def _gmlp(p, col0, v_gain, ws, bs_t, out_gain):
    m = p.shape[0]
    groups, chunk, _ = ws.shape
    half = v_gain.shape[1]
    cb = col0 // half
    assert cb * half == col0
    tm = min(512, m)
    return pl.pallas_call(
        _gmlp_kernel,
        grid=(m // tm,),
        in_specs=[
            pl.BlockSpec((tm, half), lambda i: (i, cb)),
            pl.BlockSpec((tm, half), lambda i: (i, cb + 1)),
            pl.BlockSpec((1, half), lambda i: (0, 0)),
            pl.BlockSpec((groups, chunk, chunk), lambda i: (0, 0, 0)),
            pl.BlockSpec((chunk, groups), lambda i: (0, 0)),
            pl.BlockSpec((1, half), lambda i: (0, 0)),
        ],
        out_specs=pl.BlockSpec((tm, half), lambda i: (i, 0)),
        out_shape=jax.ShapeDtypeStruct((m, half), F32),
        compiler_params=_params("parallel"),
        name="gmlp_mix",
    )(p, p, v_gain, ws, bs_t, out_gain)


def _conv_kernel(seq_len, starts, p_ref, prev_ref, cw_ref, og_ref, o_ref):
    tm = p_ref.shape[0]
    fw = o_ref.shape[1]
    sg, sc, sh = starts
    p = p_ref[...]
    gb, z = p[:, sg:sg + fw], p[:, sc:sc + fw] * p[:, sh:sh + fw]
    pv = prev_ref[...]
    seq_start = (pl.program_id(0) * tm) % seq_len == 0
    zp = jnp.where(seq_start, 0.0, pv[:, sc:sc + fw] * pv[:, sh:sh + fw])
    cw = cw_ref[...]
    taps = cw.shape[0]
    zc = cw[taps - 1:taps, :] * z
    for s in range(1, taps):
        zc = zc + cw[taps - 1 - s:taps - s, :] * _shift_rows(z, zp, s)
    y = gb * zc
    ms = _seg_sum(y * y, _group_ones(CONV_GROUP_W)) * (1.0 / CONV_GROUP_W)
    o_ref[...] = y * lax.rsqrt(ms + RMS_EPS) * og_ref[...]


def _short_conv(p, col0, conv_w, out_gain, seq_len):
    m, n_all = p.shape
    c = out_gain.shape[1]
    phase = col0 % LANES
    fw = c + LANES
    win = 4 * c
    wb = col0 // win
    assert (wb + 1) * win <= n_all and c % LANES == 0 and phase % CONV_GROUP_W == 0
    starts = tuple(col0 + j * c - phase - wb * win for j in range(3))
    assert starts[0] >= 0 and starts[2] + fw <= win
    pad = lambda a: jnp.pad(a, ((0, 0), (phase, LANES - phase)))
    tm = min(512, seq_len)
    assert seq_len % tm == 0, "a row block must not straddle two sequences"
    rb = tm // SUBLANES
    return pl.pallas_call(
        functools.partial(_conv_kernel, seq_len, starts),
        grid=(m // tm,),
        in_specs=[
            pl.BlockSpec((tm, win), lambda i: (i, wb)),
            pl.BlockSpec((SUBLANES, win), lambda i: (jnp.maximum(i * rb - 1, 0), wb)),
            pl.BlockSpec((conv_w.shape[0], fw), lambda i: (0, 0)),
            pl.BlockSpec((1, fw), lambda i: (0, 0)),
        ],
        out_specs=pl.BlockSpec((tm, fw), lambda i: (i, 0)),
        out_shape=jax.ShapeDtypeStruct((m, fw), F32),
        compiler_params=_params("parallel"),
        name="short_conv_mix",
    )(p, p, pad(conv_w), pad(out_gain))


def _rwkv_operands(p, prev_row, mu, wwa, g_up, w0, a0, k_k, k_a, n_decay):
    c = w0.shape[1]
    ps = p + (_shift_rows(p, prev_row, 1) - p) * mu
    r, k, v = ps[:, :c], ps[:, c:2 * c], ps[:, 2 * c:3 * c]
    lo = ps[:, 3 * c:3 * c + wwa.shape[0]]
    gd = ps[:, 3 * c:]
    lane = lax.broadcasted_iota(jnp.int32, lo.shape, 1)
    act = jnp.where(lane < n_decay, jnp.tanh(lo), lo)
    wa = _dot(act.astype(BF16), wwa)
    zw = w0 + wa[:, :c]
    w = -(jnp.maximum(-zw, 0.0) + jnp.log(1.0 + jnp.exp(-jnp.abs(zw)))) - 0.5
    iclr = 1.0 / (1.0 + jnp.exp(-(a0 + wa[:, c:])))
    g = _dot((1.0 / (1.0 + jnp.exp(-gd))).astype(BF16), g_up)
    kkf = k * k_k
    nrm = jnp.sqrt(_seg_sum(kkf * kkf, _group_ones(RWKV_HEAD)))
    kk = kkf / jnp.maximum(nrm, 1e-12)
    return r, k * (1.0 + (iclr - 1.0) * k_a), v, -jnp.exp(w), -kk, kk * iclr, g


def _wkv_kernel(n_decay, pr_ref, pk_ref, pv_ref, pl_ref, mu_ref, wwa_ref, gup_ref, w0_ref, a0_ref, kk_ref, ka_ref,
                lnw_ref, lnb_ref, rk_ref, o_ref, h_ref, prev_ref):
    @pl.when(pl.program_id(1) == 0)
    def _():
        h_ref[...] = jnp.zeros_like(h_ref)
        prev_ref[...] = jnp.zeros_like(prev_ref)

    TB, C = o_ref.shape
    L = RWKV_HEAD
    NCH = TB // L
    HPT = LANES // RWKV_HEAD
    CPP = 2
    NB = CPP * HPT
    p = jnp.concatenate([pr_ref[...], pk_ref[...], pv_ref[...], pl_ref[...]], axis=1)
    r, k, v, lw, a, b, g = _rwkv_operands(p, prev_ref[0:1, :], mu_ref[...], wwa_ref[...], gup_ref[...], w0_ref[...],
                                          a0_ref[...], kk_ref[...], ka_ref[...], n_decay)
    prev_ref[0:1, :] = p[TB - 1:TB, :]
    row = lax.broadcasted_iota(jnp.int32, (TB, TB), 0)
    col = lax.broadcasted_iota(jnp.int32, (TB, TB), 1)
    cum = _dot_exact_lhs(jnp.where((col <= row) & (row // L == col // L), 1.0, 0.0), lw)

    def chunk_rows(i):
        return jnp.concatenate([jnp.broadcast_to(cum[c * L + i:c * L + i + 1], (L, C)) for c in range(NCH)], axis=0)

    c_mid = chunk_rows(L // 2 - 1)
    c_end = chunk_rows(L - 1)
    e_fwd = jnp.exp(cum - c_mid)
    e_bwd = jnp.exp(c_mid - cum)
    e_end = jnp.exp(c_end - cum)
    bf = lambda z: z.astype(BF16)
    rt = bf(r * e_fwd)
    at = bf(a * jnp.exp(cum - lw - c_mid))
    bt = bf(b * e_bwd)
    kt = bf(k * e_bwd)
    bh = bf(b * e_end)
    kh = bf(k * e_end)
    vb = bf(v)

    lane = lax.broadcasted_iota(jnp.int32, (1, LANES), 1)
    hi = lax.broadcasted_iota(jnp.int32, (LANES, LANES), 0) // RWKV_HEAD
    hj = lax.broadcasted_iota(jnp.int32, (LANES, LANES), 1) // RWKV_HEAD
    same_head = hi == hj
    zero_b = jnp.zeros((), BF16)
    zeros_lt = jnp.zeros((L, LANES), BF16)

    chunks = range(NCH)
    rows = [slice(c * L, (c + 1) * L) for c in chunks]
    head_mask = [(lane // RWKV_HEAD) == h for h in range(HPT)]
    head_mask2 = [jnp.concatenate([m, m], axis=1) for m in head_mask]
    low_half = lane < RWKV_HEAD
    lr = lax.broadcasted_iota(jnp.int32, (L, LANES), 1) % L
    rr = lax.broadcasted_iota(jnp.int32, (L, LANES), 0)
    strict = lr < rr
    lower = lr <= rr
    lrp = lax.broadcasted_iota(jnp.int32, (L, NB * L), 1) % L
    rrp = lax.broadcasted_iota(jnp.int32, (L, NB * L), 0)
    blkp = lax.broadcasted_iota(jnp.int32, (L, NB * L), 1) // L
    same_blk = lambda size: (lrp // size) == (rrp // size)

    def block_diag(xp):
        return jnp.concatenate([jnp.where(blkp == q, xp, zero_b) for q in range(NB)], axis=0)

    def run_tiles(t0, nt):
        tiles = range(nt)
        tile_sl = [slice((t0 + t) * LANES, (t0 + t + 1) * LANES) for t in tiles]
        ppt = NCH // CPP
        units = [(t, p) for t in tiles for p in range(ppt)]
        uidx = range(len(units))
        lch = range(CPP)
        usl = lambda u: tile_sl[units[u][0]]
        urows = lambda u, c: rows[units[u][1] * CPP + c]

        def masked(x, u, c, h):
            return jnp.where(head_mask[h], x[urows(u, c), usl(u)], zero_b)

        am = [[[masked(at, u, c, h) for h in range(HPT)] for c in lch] for u in uidx]
        vm = [[[masked(vb, u, c, h) for h in range(HPT)] for c in lch] for u in uidx]

        def score(u, c, h):
            b_, k_ = bt[urows(u, c), usl(u)], kt[urows(u, c), usl(u)]
            rhs = jnp.concatenate([b_, k_] if h == 0 else [k_, b_], axis=0)
            return _dot_nt(jnp.concatenate([am[u][c][h], masked(rt, u, c, h)], axis=0), rhs)

        s = [[[score(u, c, h) for h in range(HPT)] for c in lch] for u in uidx]

        def pick(u, c, part, first):
            e, o = s[u][c][0][part * L:(part + 1) * L], s[u][c][1][part * L:(part + 1) * L]
            return jnp.where(low_half, e, o) if first == 0 else jnp.where(low_half, o, e)

        n_p = [jnp.concatenate([jnp.where(strict, pick(u, c, 0, 0), 0.0) for c in lch], axis=1) for u in uidx]
        ak_p = [jnp.concatenate([bf(jnp.where(strict, pick(u, c, 0, 1), 0.0)) for c in lch], axis=1) for u in uidx]
        rb = [[bf(jnp.where(lower, pick(u, c, 1, 0), 0.0)) for c in lch] for u in uidx]
        rk = [[bf(jnp.where(lower, pick(u, c, 1, 1), 0.0)) for c in lch] for u in uidx]
        eye_p = jnp.where(lrp == rrp, 1.0, 0.0)
        tinv = [eye_p + jnp.where(same_blk(2), n, 0.0) for n in n_p]
        size = 2
        while size < L:
            sub_diag = same_blk(2 * size) & jnp.logical_not(same_blk(size))
            e_bd = [block_diag(bf(jnp.where(sub_diag, n, 0.0))) for n in n_p]
            tb = [bf(ti) for ti in tinv]
            te = [bf(_dot(tb_, e_)) for tb_, e_ in zip(tb, e_bd)]
            tinv = [ti + _dot(te_, block_diag(tb_)) for ti, te_, tb_ in zip(tinv, te, tb)]
            size *= 2
        tb = [bf(ti) for ti in tinv]

        def place(x, c):
            return jnp.concatenate([x if cc == c else zeros_lt for cc in lch], axis=1)

        w = [_dot(ak_p[u], jnp.concatenate([place(vm[u][c][h], c) for c in lch for h in (1, 0)], axis=0))
             for u in uidx]

        def solve(u, c):
            wc = bf(w[u][:, c * LANES:(c + 1) * LANES])
            rhs = jnp.concatenate([jnp.concatenate([am[u][c][h], jnp.where(head_mask[h], wc, zero_b)], axis=1)
                                   for h in range(HPT)], axis=0)
            return _dot(tb[u][:, c * LANES:(c + 1) * LANES], rhs)

        xs = [[solve(u, c) for c in lch] for u in uidx]

        def readout(u, c):
            xb = bf(xs[u][c])
            rhs = jnp.concatenate([jnp.where(head_mask2[0], xb, zero_b), jnp.where(head_mask2[1], xb, zero_b),
                                   jnp.concatenate([zeros_lt, vm[u][c][1]], axis=1),
                                   jnp.concatenate([zeros_lt, vm[u][c][0]], axis=1)], axis=0)
            return _dot(jnp.concatenate([rb[u][c], rk[u][c]], axis=1), rhs)

        zs = [[readout(u, c) for c in lch] for u in uidx]
        x = [[xs[t * ppt + c // CPP][c % CPP] for c in chunks] for t in tiles]
        z = [[zs[t * ppt + c // CPP][c % CPP] for c in chunks] for t in tiles]

        h = [h_ref[t0 + t] for t in tiles]
        y_rows = [[None] * NCH for _ in tiles]
        for c in chunks:
            dec_mid = jnp.exp(cum[c * L + L // 2 - 1:c * L + L // 2, :])
            dec = jnp.exp(cum[c * L + L - 1:c * L + L, :])
            h_mid = [bf(h[t] * jnp.transpose(jnp.broadcast_to(dec_mid[:, tile_sl[t]], (LANES, LANES)))) for t in tiles]
            lhs = [bf(jnp.concatenate([x[t][c][:, :LANES],
                                       rt[rows[c], tile_sl[t]].astype(F32) + z[t][c][:, :LANES]], axis=0)) for t in tiles]
            ah = [_dot(lhs[t], h_mid[t]) for t in tiles]
            u = [ah[t][:L] + x[t][c][:, LANES:] for t in tiles]
            for t in tiles:
                y_rows[t][c] = ah[t][L:] + z[t][c][:, LANES:]
            upd = [_dot_tn(jnp.concatenate([bh[rows[c], tile_sl[t]], kh[rows[c], tile_sl[t]]], axis=0),
                           jnp.concatenate([bf(u[t]), vb[rows[c], tile_sl[t]]], axis=0)) for t in tiles]
            h = [h[t] * jnp.transpose(jnp.broadcast_to(dec[:, tile_sl[t]], (LANES, LANES)))
                 + jnp.where(same_head, upd[t], 0.0) for t in tiles]
        for t in tiles:
            h_ref[t0 + t] = h[t]
        return [jnp.concatenate(y_rows[t], axis=0) for t in tiles]

    n_tiles = C // LANES
    y_tiles = []
    for t0 in range(0, n_tiles, WKV_TILE_GROUP):
        y_tiles += run_tiles(t0, min(WKV_TILE_GROUP, n_tiles - t0))
    y = jnp.concatenate(y_tiles, axis=1)
    ones_bd = _group_ones(RWKV_HEAD)
    inv_n = 1.0 / RWKV_HEAD
    mean = _seg_sum(y, ones_bd) * inv_n
    d = y - mean
    var = _seg_sum(d * d, ones_bd) * inv_n
    yn = d * lax.rsqrt(var + LNX_EPS) * lnw_ref[...] + lnb_ref[...]
    bonus = _seg_sum(r * k * rk_ref[...], ones_bd) * v
    o_ref[...] = (yn + bonus) * g


def _rwkv_mix(p, col0, mu, wwa, g_up, w0, a0, k_k, k_a, ln_w, ln_b, r_k, batch, seq_len, n_decay):
    m = p.shape[0]
    nb = mu.shape[1]
    c = w0.shape[1]
    lw_ = g_up.shape[0]
    assert col0 % c == 0 and (col0 + 3 * c) % lw_ == 0 and nb == 3 * c + lw_
    cb, lb = col0 // c, (col0 + 3 * c) // lw_
    tb = min(WKV_STEP, seq_len)
    assert LANES == 2 * RWKV_HEAD and RWKV_HEAD & (RWKV_HEAD - 1) == 0, "kernel packs two power-of-two heads per lane tile"
    assert tb % (2 * RWKV_HEAD) == 0 and seq_len % tb == 0 and c % LANES == 0
    nc = seq_len // tb
    vec = lambda n: pl.BlockSpec((1, n), lambda bi, ci: (0, 0))
    full = lambda a: pl.BlockSpec(a.shape, lambda bi, ci: (0, 0))
    return pl.pallas_call(
        functools.partial(_wkv_kernel, n_decay),
        grid=(batch, nc),
        in_specs=[pl.BlockSpec((tb, c), lambda bi, ci: (bi * nc + ci, cb)),
                  pl.BlockSpec((tb, c), lambda bi, ci: (bi * nc + ci, cb + 1)),
                  pl.BlockSpec((tb, c), lambda bi, ci: (bi * nc + ci, cb + 2)),
                  pl.BlockSpec((tb, lw_), lambda bi, ci: (bi * nc + ci, lb)),
                  vec(nb), full(wwa), full(g_up)] + [vec(c)] * 7,
        out_specs=pl.BlockSpec((tb, c), lambda bi, ci: (bi * nc + ci, 0)),
        out_shape=jax.ShapeDtypeStruct((m, c), F32),
        scratch_shapes=[pltpu.VMEM((c // LANES, LANES, LANES), F32), pltpu.VMEM((SUBLANES, nb), F32)],
        compiler_params=_params("parallel", "arbitrary"),
        name="rwkv7_mix",
    )(p, p, p, p, mu, wwa, g_up, w0, a0, k_k, k_a, ln_w, ln_b, r_k)


def _out_proj_kernel(phase, ya_ref, yb_ref, yc_ref, w_ref, g_ref, x_ref, o_ref, wb_ref):
    @pl.when(pl.program_id(0) == 0)
    def _():
        wb_ref[...] = w_ref[...].astype(BF16)

    ra = ya_ref.shape[1]
    rc = ra + yb_ref.shape[1]
    rows, d = wb_ref.shape
    start = rc - phase
    full = (rows - start) // LANES * LANES
    yc = yc_ref[...].astype(BF16)
    w_tail = jnp.concatenate([wb_ref[start + full:rows, :],
                              jnp.zeros((LANES - (rows - start - full), d), BF16)], axis=0)
    y = (_dot(ya_ref[...].astype(BF16), wb_ref[0:ra, :]) + _dot(yb_ref[...].astype(BF16), wb_ref[ra:rc, :])
         + _dot(yc[:, :full], wb_ref[start:start + full, :]) + _dot(yc[:, full:], w_tail))
    o_ref[...] = x_ref[...] + _rms(y, g_ref[...])


def _out_proj(ya, yb, yc, phase, w_all, layer, gain, x):
    m, d = x.shape
    rows = w_all.shape[1]
    tm = min(256, m)
    start = ya.shape[1] + yb.shape[1] - phase
    assert yc.shape[1] == (rows - start) // LANES * LANES + LANES and 0 < (rows - start) % LANES
    act = lambda a: pl.BlockSpec((tm, a.shape[1]), lambda i: (i, 0))
    return pl.pallas_call(
        functools.partial(_out_proj_kernel, phase),
        grid=(m // tm,),
        in_specs=[act(ya), act(yb), act(yc),
                  pl.BlockSpec((None, rows, d), lambda i: (layer, 0, 0), pipeline_mode=pl.Buffered(1)),
                  pl.BlockSpec((1, d), lambda i: (0, 0)), act(x)],
        out_specs=act(x),
        out_shape=jax.ShapeDtypeStruct((m, d), F32),
        scratch_shapes=[pltpu.VMEM((rows, d), BF16)],
        compiler_params=_params("arbitrary"),
        name="out_proj",
    )(ya, yb, yc, w_all, gain, x)


def _mlp_kernel(x_ref, gpre_ref, wu_ref, wd_ref, gpost_ref, o_ref, h_ref):
    j = pl.program_id(1)

    @pl.when(j == 0)
    def _():
        h_ref[...] = _rms(x_ref[...], gpre_ref[...]).astype(BF16)
        o_ref[...] = jnp.zeros_like(o_ref)

    f = jnp.maximum(_dot(h_ref[...], wu_ref[...].astype(BF16)), 0.0)
    o_ref[...] += _dot((f * f).astype(BF16), wd_ref[...].astype(BF16))

    @pl.when(j == pl.num_programs(1) - 1)
    def _():
        o_ref[...] = x_ref[...] + _rms(o_ref[...], gpost_ref[...])


def _mlp(x, g_pre, w_up_all, w_down_all, g_post, layer):
    m, d = x.shape
    ff = w_up_all.shape[2]
    tm = min(1024, m)
    tf = min(512, ff)
    return pl.pallas_call(
        _mlp_kernel,
        grid=(m // tm, ff // tf),
        in_specs=[
            pl.BlockSpec((tm, d), lambda i, j: (i, 0)),
            pl.BlockSpec((1, d), lambda i, j: (0, 0)),
            pl.BlockSpec((None, d, tf), lambda i, j: (layer, 0, j)),
            pl.BlockSpec((None, tf, d), lambda i, j: (layer, j, 0)),
            pl.BlockSpec((1, d), lambda i, j: (0, 0)),
        ],
        out_specs=pl.BlockSpec((tm, d), lambda i, j: (i, 0)),
        out_shape=jax.ShapeDtypeStruct((m, d), F32),
        scratch_shapes=[pltpu.VMEM((tm, d), BF16)],
        compiler_params=_params("parallel", "arbitrary", vmem=VMEM_LIMIT_MLP),
        name="relu2_mlp",
    )(x, g_pre, w_up_all, w_down_all, g_post)


def kernel(x, norm_mix_pre, norm_mix_post, norm_mlp_pre, norm_mlp_post, w_in, gm_v_gain, gm_ws, gm_bs,
           gm_out_gain, rk_mu, rk_w0, rk_w_up, rk_a0, rk_a_up, rk_g_up, rk_k_k, rk_k_a, rk_r_k, rk_ln_w,
           rk_ln_b, sc_conv, sc_out_gain, w_out, mlp_up, mlp_down):
    batch, seq_len, d = x.shape
    depth = w_in.shape[0]
    mix_a = gm_v_gain.shape[1]
    mix_b = rk_w0.shape[1]
    mix_c = sc_out_gain.shape[1]
    n_decay, n_iclr, n_gate = rk_w_up.shape[1], rk_a_up.shape[1], rk_g_up.shape[1]
    n_lora = n_decay + n_iclr
    lora_pad = -(-n_lora // LANES) * LANES
    col_b = 2 * mix_a
    col_lora = col_b + 3 * mix_b
    col_gate = col_lora + n_lora
    col_c = col_gate + n_gate
    row = lambda a: a.reshape(1, -1)

    xf = x.reshape(batch * seq_len, d)
    n_in = w_in.shape[2]
    n_proj = -(-n_in // PROJ_TILE_N) * PROJ_TILE_N
    w_proj = jnp.concatenate([w_in.astype(BF16), jnp.zeros((depth, d, n_proj - n_in), BF16)], axis=2)
    lora_blk = -(-(n_lora + n_gate) // LANES) * LANES
    conv_phase = col_c % LANES
    for l in range(depth):
        mu_b = jnp.pad(rk_mu[l], (0, 3 * mix_b + lora_blk - rk_mu.shape[1])).reshape(1, -1)
        wwa = jnp.zeros((lora_pad, 2 * mix_b), F32)
        wwa = wwa.at[:n_decay, :mix_b].set(rk_w_up[l]).at[n_decay:n_lora, mix_b:].set(rk_a_up[l]).astype(BF16)
        g_up = jnp.zeros((lora_blk, mix_b), F32).at[n_lora:n_lora + n_gate].set(rk_g_up[l]).astype(BF16)

        p = _norm_matmul(xf, row(norm_mix_pre[l]), w_proj, l)

        ya = _gmlp(p, 0, row(gm_v_gain[l]), gm_ws[l], gm_bs[l].T, row(gm_out_gain[l]))
        yb = _rwkv_mix(p, col_b, mu_b, wwa, g_up, row(rk_w0[l]), row(rk_a0[l]), row(rk_k_k[l]),
                       row(rk_k_a[l]), row(rk_ln_w[l]), row(rk_ln_b[l]), row(rk_r_k[l]), batch, seq_len, n_decay)
        yc = _short_conv(p, col_c, sc_conv[l], row(sc_out_gain[l]), seq_len)

        xf = _out_proj(ya, yb, yc, conv_phase, w_out, l, row(norm_mix_post[l]), xf)
        xf = _mlp(xf, row(norm_mlp_pre[l]), mlp_up, mlp_down, row(norm_mlp_post[l]), l)
    return xf.reshape(batch, seq_len, d)
```

```python
import functools

import jax
import jax.numpy as jnp
from jax import lax
from jax.experimental import pallas as pl
from jax.experimental.pallas import tpu as pltpu

F32 = jnp.float32
BF16 = jnp.bfloat16

RMS_EPS = 1e-6
LNX_EPS = 64e-5
RWKV_HEAD = 64
CONV_GROUP_W = 64
LANES = 128
SUBLANES = 8
WKV_STEP = 128
PROJ_TILE_N = 1024
WKV_TILE_GROUP = 8
V7X_VMEM_BYTES = 64 * 1024 * 1024
VMEM_LIMIT = 56 * 1024 * 1024
VMEM_LIMIT_MLP = V7X_VMEM_BYTES - 2 * 1024 * 1024


def _params(*sem, vmem=VMEM_LIMIT):
    return pltpu.CompilerParams(dimension_semantics=sem, vmem_limit_bytes=vmem)


def _rms(x, gain):
    return x * lax.rsqrt(jnp.mean(x * x, axis=-1, keepdims=True) + RMS_EPS) * gain


def _dot(a, b):
    return jnp.dot(a, b, preferred_element_type=F32)


def _dot_exact_lhs(a, b):
    b0 = b.astype(BF16)
    r1 = b - b0.astype(F32)
    b1 = r1.astype(BF16)
    b2 = (r1 - b1.astype(F32)).astype(BF16)
    ab = a.astype(BF16)
    return _dot(jnp.concatenate([ab, ab, ab], axis=1), jnp.concatenate([b0, b1, b2], axis=0))


def _dot_nt(a, b):
    return lax.dot_general(a, b, (((1,), (1,)), ((), ())), preferred_element_type=F32)


def _dot_tn(a, b):
    return lax.dot_general(a, b, (((0,), (0,)), ((), ())), preferred_element_type=F32)


def _group_ones(width):
    i = (lax.broadcasted_iota(jnp.int32, (2 * LANES, LANES), 0) % LANES) // width
    j = lax.broadcasted_iota(jnp.int32, (2 * LANES, LANES), 1) // width
    return jnp.where(i == j, 1.0, 0.0).astype(BF16)


def _seg_sum(x, ones2):
    hi = x.astype(BF16)
    lo = (x - hi.astype(F32)).astype(BF16)
    cols = []
    for c in range(x.shape[-1] // LANES):
        sl = slice(c * LANES, (c + 1) * LANES)
        cols.append(_dot(jnp.concatenate([hi[:, sl], lo[:, sl]], axis=1), ones2))
    return cols[0] if len(cols) == 1 else jnp.concatenate(cols, axis=-1)


def _shift_rows(x, prev_rows, n):
    rolled = pltpu.roll(x, n, axis=0)
    row = lax.broadcasted_iota(jnp.int32, x.shape, 0)
    p = prev_rows.shape[0]
    for j in range(n):
        rolled = jnp.where(row == j, prev_rows[p - n + j:p - n + j + 1, :], rolled)
    return rolled


def _norm_matmul_kernel(n_full, x_ref, g_ref, w_ref, wt_ref, o_ref, h_ref):
    j = pl.program_id(1)

    @pl.when(j == 0)
    def _():
        h_ref[...] = _rms(x_ref[...], g_ref[...]).astype(BF16)

    @pl.when(j < n_full)
    def _():
        o_ref[...] = _dot(h_ref[...], w_ref[...])

    @pl.when(j >= n_full)
    def _():
        o_ref[...] = _dot(h_ref[...], wt_ref[...])


def _norm_matmul(x, gain, w_all, w_tail, layer):
    m, d = x.shape
    tm = min(1024, m)
    tn = w_tail.shape[2]
    n_full = w_all.shape[2] // tn
    return pl.pallas_call(
        functools.partial(_norm_matmul_kernel, n_full),
        grid=(m // tm, n_full + 1),
        in_specs=[
            pl.BlockSpec((tm, d), lambda i, j: (i, 0)),
            pl.BlockSpec((1, d), lambda i, j: (0, 0)),
            pl.BlockSpec((None, d, tn), lambda i, j: (layer, 0, jnp.minimum(j, n_full - 1))),
            pl.BlockSpec((None, d, tn), lambda i, j: (layer, 0, 0), pipeline_mode=pl.Buffered(1)),
        ],
        out_specs=pl.BlockSpec((tm, tn), lambda i, j: (i, j)),
        out_shape=jax.ShapeDtypeStruct((m, (n_full + 1) * tn), F32),
        scratch_shapes=[pltpu.VMEM((tm, d), BF16)],
        compiler_params=_params("parallel", "arbitrary"),
        name="in_proj",
    )(x, gain, w_all, w_tail)


def _gelu_tanh(x):
    return 0.5 * x * (1.0 + jnp.tanh(0.7978845608028654 * (x + 0.044715 * (x * x * x))))


def _gmlp_kernel(pu_ref, pv_ref, vg_ref, ws_ref, bst_ref, og_ref, o_ref):
    groups, chunk, _ = ws_ref.shape
    tm = pu_ref.shape[0]
    zu = _gelu_tanh(pu_ref[...])
    zv = _gelu_tanh(pv_ref[...])
    row = lax.broadcasted_iota(jnp.int32, (chunk, chunk), 0)
    col = lax.broadcasted_iota(jnp.int32, (chunk, chunk), 1)
    bst = bst_ref[...]
    for g in range(groups):
        sl = slice(g * LANES, (g + 1) * LANES)
        u = zu[:, sl]
        v = _rms(zv[:, sl], vg_ref[:, sl]).astype(BF16)
        w = jnp.where(col <= row, ws_ref[g], 0.0).astype(BF16)
        bias = bst[:, g:g + 1]
        for c in range(tm // chunk):
            rows = slice(c * chunk, (c + 1) * chunk)
            mixed = _dot(w, v[rows]) + bias
            o_ref[rows, sl] = _rms(u[rows] * mixed, og_ref[:, sl])


def _gmlp(p, col0, v_gain, ws, bs_t, out_gain):
    m = p.shape[0]
    groups, chunk, _ = ws.shape
    half = v_gain.shape[1]
    cb = col0 // half
    assert cb * half == col0
    tm = min(512, m)
    return pl.pallas_call(
        _gmlp_kernel,
        grid=(m // tm,),
        in_specs=[
            pl.BlockSpec((tm, half), lambda i: (i, cb)),
            pl.BlockSpec((tm, half), lambda i: (i, cb + 1)),
            pl.BlockSpec((1, half), lambda i: (0, 0)),
            pl.BlockSpec((groups, chunk, chunk), lambda i: (0, 0, 0)),
            pl.BlockSpec((chunk, groups), lambda i: (0, 0)),
            pl.BlockSpec((1, half), lambda i: (0, 0)),
        ],
        out_specs=pl.BlockSpec((tm, half), lambda i: (i, 0)),
        out_shape=jax.ShapeDtypeStruct((m, half), F32),
        compiler_params=_params("parallel"),
        name="gmlp_mix",
    )(p, p, v_gain, ws, bs_t, out_gain)


def _conv_kernel(seq_len, starts, p_ref, prev_ref, cw_ref, og_ref, o_ref):
    tm = p_ref.shape[0]
    fw = o_ref.shape[1]
    sg, sc, sh = starts
    p = p_ref[...]
    gb, z = p[:, sg:sg + fw], p[:, sc:sc + fw] * p[:, sh:sh + fw]
    pv = prev_ref[...]
    seq_start = (pl.program_id(0) * tm) % seq_len == 0
    zp = jnp.where(seq_start, 0.0, pv[:, sc:sc + fw] * pv[:, sh:sh + fw])
    cw = cw_ref[...]
    taps = cw.shape[0]
    zc = cw[taps - 1:taps, :] * z
    for s in range(1, taps):
        zc = zc + cw[taps - 1 - s:taps - s, :] * _shift_rows(z, zp, s)
    y = gb * zc
    ms = _seg_sum(y * y, _group_ones(CONV_GROUP_W)) * (1.0 / CONV_GROUP_W)
    o_ref[...] = y * lax.rsqrt(ms + RMS_EPS) * og_ref[...]


def _short_conv(p, col0, conv_w, out_gain, seq_len):
    m, n_all = p.shape
    c = out_gain.shape[1]
    phase = col0 % LANES
    fw = c + LANES
    win = 4 * c
    wb = col0 // win
    assert (wb + 1) * win <= n_all and c % LANES == 0 and phase % CONV_GROUP_W == 0
    starts = tuple(col0 + j * c - phase - wb * win for j in range(3))
    assert starts[0] >= 0 and starts[2] + fw <= win
    pad = lambda a: jnp.pad(a, ((0, 0), (phase, LANES - phase)))
    tm = min(512, seq_len)
    assert seq_len % tm == 0, "a row block must not straddle two sequences"
    rb = tm // SUBLANES
    return pl.pallas_call(
        functools.partial(_conv_kernel, seq_len, starts),
        grid=(m // tm,),
        in_specs=[
            pl.BlockSpec((tm, win), lambda i: (i, wb)),
            pl.BlockSpec((SUBLANES, win), lambda i: (jnp.maximum(i * rb - 1, 0), wb)),
            pl.BlockSpec((conv_w.shape[0], fw), lambda i: (0, 0)),
            pl.BlockSpec((1, fw), lambda i: (0, 0)),
        ],
        out_specs=pl.BlockSpec((tm, fw), lambda i: (i, 0)),
        out_shape=jax.ShapeDtypeStruct((m, fw), F32),
        compiler_params=_params("parallel"),
        name="short_conv_mix",
    )(p, p, pad(conv_w), pad(out_gain))


def _rwkv_operands(p, prev_row, mu, wwa, g_up, w0, a0, k_k, k_a, n_decay):
    c = w0.shape[1]
    ps = p + (_shift_rows(p, prev_row, 1) - p) * mu
    r, k, v = ps[:, :c], ps[:, c:2 * c], ps[:, 2 * c:3 * c]
    lo = ps[:, 3 * c:3 * c + wwa.shape[0]]
    gd = ps[:, 3 * c:]
    lane = lax.broadcasted_iota(jnp.int32, lo.shape, 1)
    act = jnp.where(lane < n_decay, jnp.tanh(lo), lo)
    wa = _dot(act.astype(BF16), wwa)
    zw = w0 + wa[:, :c]
    w = -(jnp.maximum(-zw, 0.0) + jnp.log(1.0 + jnp.exp(-jnp.abs(zw)))) - 0.5
    iclr = 1.0 / (1.0 + jnp.exp(-(a0 + wa[:, c:])))
    g = _dot((1.0 / (1.0 + jnp.exp(-gd))).astype(BF16), g_up)
    kkf = k * k_k
    nrm = jnp.sqrt(_seg_sum(kkf * kkf, _group_ones(RWKV_HEAD)))
    kk = kkf / jnp.maximum(nrm, 1e-12)
    return r, k * (1.0 + (iclr - 1.0) * k_a), v, -jnp.exp(w), -kk, kk * iclr, g


def _wkv_kernel(n_decay, pr_ref, pk_ref, pv_ref, pl_ref, mu_ref, wwa_ref, gup_ref, w0_ref, a0_ref, kk_ref, ka_ref,
                lnw_ref, lnb_ref, rk_ref, o_ref, h_ref, prev_ref):
    @pl.when(pl.program_id(1) == 0)
    def _():
        h_ref[...] = jnp.zeros_like(h_ref)
        prev_ref[...] = jnp.zeros_like(prev_ref)

    TB, C = o_ref.shape
    L = RWKV_HEAD
    NCH = TB // L
    HPT = LANES // RWKV_HEAD
    CPP = 2
    NB = CPP * HPT
    p = jnp.concatenate([pr_ref[...], pk_ref[...], pv_ref[...], pl_ref[...]], axis=1)
    r, k, v, lw, a, b, g = _rwkv_operands(p, prev_ref[0:1, :], mu_ref[...], wwa_ref[...], gup_ref[...], w0_ref[...],
                                          a0_ref[...], kk_ref[...], ka_ref[...], n_decay)
    prev_ref[0:1, :] = p[TB - 1:TB, :]
    row = lax.broadcasted_iota(jnp.int32, (TB, TB), 0)
    col = lax.broadcasted_iota(jnp.int32, (TB, TB), 1)
    cum = _dot_exact_lhs(jnp.where((col <= row) & (row // L == col // L), 1.0, 0.0), lw)

    def chunk_rows(i):
        return jnp.concatenate([jnp.broadcast_to(cum[c * L + i:c * L + i + 1], (L, C)) for c in range(NCH)], axis=0)

    c_mid = chunk_rows(L // 2 - 1)
    c_end = chunk_rows(L - 1)
    e_fwd = jnp.exp(cum - c_mid)
    e_bwd = jnp.exp(c_mid - cum)
    e_end = jnp.exp(c_end - cum)
    bf = lambda z: z.astype(BF16)
    rt = bf(r * e_fwd)
    at = bf(a * jnp.exp(cum - lw - c_mid))
    bt = bf(b * e_bwd)
    kt = bf(k * e_bwd)
    bh = bf(b * e_end)
    kh = bf(k * e_end)
    vb = bf(v)

    lane = lax.broadcasted_iota(jnp.int32, (1, LANES), 1)
    hi = lax.broadcasted_iota(jnp.int32, (LANES, LANES), 0) // RWKV_HEAD
    hj = lax.broadcasted_iota(jnp.int32, (LANES, LANES), 1) // RWKV_HEAD
    same_head = hi == hj
    zero_b = jnp.zeros((), BF16)
    zeros_lt = jnp.zeros((L, LANES), BF16)

    chunks = range(NCH)
    rows = [slice(c * L, (c + 1) * L) for c in chunks]
    head_mask = [(lane // RWKV_HEAD) == h for h in range(HPT)]
    head_mask2 = [jnp.concatenate([m, m], axis=1) for m in head_mask]
    low_half = lane < RWKV_HEAD
    lr = lax.broadcasted_iota(jnp.int32, (L, LANES), 1) % L
    rr = lax.broadcasted_iota(jnp.int32, (L, LANES), 0)
    strict = lr < rr
    lower = lr <= rr
    lrp = lax.broadcasted_iota(jnp.int32, (L, NB * L), 1) % L
    rrp = lax.broadcasted_iota(jnp.int32, (L, NB * L), 0)
    blkp = lax.broadcasted_iota(jnp.int32, (L, NB * L), 1) // L
    same_blk = lambda size: (lrp // size) == (rrp // size)

    def block_diag(xp):
        return jnp.concatenate([jnp.where(blkp == q, xp, zero_b) for q in range(NB)], axis=0)

    def run_tiles(t0, nt):
        tiles = range(nt)
        tile_sl = [slice((t0 + t) * LANES, (t0 + t + 1) * LANES) for t in tiles]
        ppt = NCH // CPP
        units = [(t, p) for t in tiles for p in range(ppt)]
        uidx = range(len(units))
        lch = range(CPP)
        usl = lambda u: tile_sl[units[u][0]]
        urows = lambda u, c: rows[units[u][1] * CPP + c]

        def masked(x, u, c, h):
            return jnp.where(head_mask[h], x[urows(u, c), usl(u)], zero_b)

        am = [[[masked(at, u, c, h) for h in range(HPT)] for c in lch] for u in uidx]
        vm = [[[masked(vb, u, c, h) for h in range(HPT)] for c in lch] for u in uidx]

        def score(u, c, h):
            b_, k_ = bt[urows(u, c), usl(u)], kt[urows(u, c), usl(u)]
            rhs = jnp.concatenate([b_, k_] if h == 0 else [k_, b_], axis=0)
            return _dot_nt(jnp.concatenate([am[u][c][h], masked(rt, u, c, h)], axis=0), rhs)

        s = [[[score(u, c, h) for h in range(HPT)] for c in lch] for u in uidx]

        def pick(u, c, part, first):
            e, o = s[u][c][0][part * L:(part + 1) * L], s[u][c][1][part * L:(part + 1) * L]
            return jnp.where(low_half, e, o) if first == 0 else jnp.where(low_half, o, e)

        n_p = [jnp.concatenate([jnp.where(strict, pick(u, c, 0, 0), 0.0) for c in lch], axis=1) for u in uidx]
        ak_p = [jnp.concatenate([bf(jnp.where(strict, pick(u, c, 0, 1), 0.0)) for c in lch], axis=1) for u in uidx]
        rb = [[bf(jnp.where(lower, pick(u, c, 1, 0), 0.0)) for c in lch] for u in uidx]
        rk = [[bf(jnp.where(lower, pick(u, c, 1, 1), 0.0)) for c in lch] for u in uidx]
        eye_p = jnp.where(lrp == rrp, 1.0, 0.0)
        tinv = [eye_p + jnp.where(same_blk(2), n, 0.0) for n in n_p]
        size = 2
        while size < L:
            sub_diag = same_blk(2 * size) & jnp.logical_not(same_blk(size))
            e_bd = [block_diag(bf(jnp.where(sub_diag, n, 0.0))) for n in n_p]
            tb = [bf(ti) for ti in tinv]
            te = [bf(_dot(tb_, e_)) for tb_, e_ in zip(tb, e_bd)]
            tinv = [ti + _dot(te_, block_diag(tb_)) for ti, te_, tb_ in zip(tinv, te, tb)]
            size *= 2
        tb = [bf(ti) for ti in tinv]

        def place(x, c):
            return jnp.concatenate([x if cc == c else zeros_lt for cc in lch], axis=1)

        w = [_dot(ak_p[u], jnp.concatenate([place(vm[u][c][h], c) for c in lch for h in (1, 0)], axis=0))
             for u in uidx]

        def solve(u, c):
            wc = bf(w[u][:, c * LANES:(c + 1) * LANES])
            rhs = jnp.concatenate([jnp.concatenate([am[u][c][h], jnp.where(head_mask[h], wc, zero_b)], axis=1)
                                   for h in range(HPT)], axis=0)
            return _dot(tb[u][:, c * LANES:(c + 1) * LANES], rhs)

        xs = [[solve(u, c) for c in lch] for u in uidx]

        def readout(u, c):
            xb = bf(xs[u][c])
            rhs = jnp.concatenate([jnp.where(head_mask2[0], xb, zero_b), jnp.where(head_mask2[1], xb, zero_b),
                                   jnp.concatenate([zeros_lt, vm[u][c][1]], axis=1),
                                   jnp.concatenate([zeros_lt, vm[u][c][0]], axis=1)], axis=0)
            return _dot(jnp.concatenate([rb[u][c], rk[u][c]], axis=1), rhs)

        zs = [[readout(u, c) for c in lch] for u in uidx]
        x = [[xs[t * ppt + c // CPP][c % CPP] for c in chunks] for t in tiles]
        z = [[zs[t * ppt + c // CPP][c % CPP] for c in chunks] for t in tiles]

        h = [h_ref[t0 + t] for t in tiles]
        y_rows = [[None] * NCH for _ in tiles]
        for c in chunks:
            dec_mid = jnp.exp(cum[c * L + L // 2 - 1:c * L + L // 2, :])
            dec = jnp.exp(cum[c * L + L - 1:c * L + L, :])
            h_mid = [bf(h[t] * jnp.transpose(jnp.broadcast_to(dec_mid[:, tile_sl[t]], (LANES, LANES)))) for t in tiles]
            lhs = [bf(jnp.concatenate([x[t][c][:, :LANES],
                                       rt[rows[c], tile_sl[t]].astype(F32) + z[t][c][:, :LANES]], axis=0)) for t in tiles]
            ah = [_dot(lhs[t], h_mid[t]) for t in tiles]
            u = [ah[t][:L] + x[t][c][:, LANES:] for t in tiles]
            for t in tiles:
                y_rows[t][c] = ah[t][L:] + z[t][c][:, LANES:]
            upd = [_dot_tn(jnp.concatenate([bh[rows[c], tile_sl[t]], kh[rows[c], tile_sl[t]]], axis=0),
                           jnp.concatenate([bf(u[t]), vb[rows[c], tile_sl[t]]], axis=0)) for t in tiles]
            h = [h[t] * jnp.transpose(jnp.broadcast_to(dec[:, tile_sl[t]], (LANES, LANES)))
                 + jnp.where(same_head, upd[t], 0.0) for t in tiles]
        for t in tiles:
            h_ref[t0 + t] = h[t]
        return [jnp.concatenate(y_rows[t], axis=0) for t in tiles]

    n_tiles = C // LANES
    y_tiles = []
    for t0 in range(0, n_tiles, WKV_TILE_GROUP):
        y_tiles += run_tiles(t0, min(WKV_TILE_GROUP, n_tiles - t0))
    y = jnp.concatenate(y_tiles, axis=1)
    ones_bd = _group_ones(RWKV_HEAD)
    inv_n = 1.0 / RWKV_HEAD
    mean = _seg_sum(y, ones_bd) * inv_n
    d = y - mean
    var = _seg_sum(d * d, ones_bd) * inv_n
    yn = d * lax.rsqrt(var + LNX_EPS) * lnw_ref[...] + lnb_ref[...]
    bonus = _seg_sum(r * k * rk_ref[...], ones_bd) * v
    o_ref[...] = (yn + bonus) * g


def _rwkv_mix(p, col0, mu, wwa, g_up, w0, a0, k_k, k_a, ln_w, ln_b, r_k, batch, seq_len, n_decay):
    m = p.shape[0]
    nb = mu.shape[1]
    c = w0.shape[1]
    lw_ = g_up.shape[0]
    assert col0 % c == 0 and (col0 + 3 * c) % lw_ == 0 and nb == 3 * c + lw_
    cb, lb = col0 // c, (col0 + 3 * c) // lw_
    tb = min(WKV_STEP, seq_len)
    assert LANES == 2 * RWKV_HEAD and RWKV_HEAD & (RWKV_HEAD - 1) == 0, "kernel packs two power-of-two heads per lane tile"
    assert tb % (2 * RWKV_HEAD) == 0 and seq_len % tb == 0 and c % LANES == 0
    nc = seq_len // tb
    vec = lambda n: pl.BlockSpec((1, n), lambda bi, ci: (0, 0))
    full = lambda a: pl.BlockSpec(a.shape, lambda bi, ci: (0, 0))
    return pl.pallas_call(
        functools.partial(_wkv_kernel, n_decay),
        grid=(batch, nc),
        in_specs=[pl.BlockSpec((tb, c), lambda bi, ci: (bi * nc + ci, cb)),
                  pl.BlockSpec((tb, c), lambda bi, ci: (bi * nc + ci, cb + 1)),
                  pl.BlockSpec((tb, c), lambda bi, ci: (bi * nc + ci, cb + 2)),
                  pl.BlockSpec((tb, lw_), lambda bi, ci: (bi * nc + ci, lb)),
                  vec(nb), full(wwa), full(g_up)] + [vec(c)] * 7,
        out_specs=pl.BlockSpec((tb, c), lambda bi, ci: (bi * nc + ci, 0)),
        out_shape=jax.ShapeDtypeStruct((m, c), F32),
        scratch_shapes=[pltpu.VMEM((c // LANES, LANES, LANES), F32), pltpu.VMEM((SUBLANES, nb), F32)],
        compiler_params=_params("parallel", "arbitrary"),
        name="rwkv7_mix",
    )(p, p, p, p, mu, wwa, g_up, w0, a0, k_k, k_a, ln_w, ln_b, r_k)


def _out_proj_kernel(phase, ya_ref, yb_ref, yc_ref, w_ref, g_ref, x_ref, o_ref, wb_ref):
    @pl.when(pl.program_id(0) == 0)
    def _():
        wb_ref[...] = w_ref[...].astype(BF16)

    ra = ya_ref.shape[1]
    rc = ra + yb_ref.shape[1]
    rows, d = wb_ref.shape
    start = rc - phase
    full = (rows - start) // LANES * LANES
    yc = yc_ref[...].astype(BF16)
    w_tail = jnp.concatenate([wb_ref[start + full:rows, :],
                              jnp.zeros((LANES - (rows - start - full), d), BF16)], axis=0)
    y = (_dot(ya_ref[...].astype(BF16), wb_ref[0:ra, :]) + _dot(yb_ref[...].astype(BF16), wb_ref[ra:rc, :])
         + _dot(yc[:, :full], wb_ref[start:start + full, :]) + _dot(yc[:, full:], w_tail))
    o_ref[...] = x_ref[...] + _rms(y, g_ref[...])


def _out_proj(ya, yb, yc, phase, w_all, layer, gain, x):
    m, d = x.shape
    rows = w_all.shape[1]
    tm = min(256, m)
    start = ya.shape[1] + yb.shape[1] - phase
    assert yc.shape[1] == (rows - start) // LANES * LANES + LANES and 0 < (rows - start) % LANES
    act = lambda a: pl.BlockSpec((tm, a.shape[1]), lambda i: (i, 0))
    return pl.pallas_call(
        functools.partial(_out_proj_kernel, phase),
        grid=(m // tm,),
        in_specs=[act(ya), act(yb), act(yc),
                  pl.BlockSpec((None, rows, d), lambda i: (layer, 0, 0), pipeline_mode=pl.Buffered(1)),
                  pl.BlockSpec((1, d), lambda i: (0, 0)), act(x)],
        out_specs=act(x),
        out_shape=jax.ShapeDtypeStruct((m, d), F32),
        scratch_shapes=[pltpu.VMEM((rows, d), BF16)],
        compiler_params=_params("arbitrary"),
        name="out_proj",
    )(ya, yb, yc, w_all, gain, x)


def _mlp_kernel(x_ref, gpre_ref, wu_ref, wd_ref, gpost_ref, o_ref, h_ref):
    j = pl.program_id(1)

    @pl.when(j == 0)
    def _():
        h_ref[...] = _rms(x_ref[...], gpre_ref[...]).astype(BF16)
        o_ref[...] = jnp.zeros_like(o_ref)

    f = jnp.maximum(_dot(h_ref[...], wu_ref[...].astype(BF16)), 0.0)
    o_ref[...] += _dot((f * f).astype(BF16), wd_ref[...].astype(BF16))

    @pl.when(j == pl.num_programs(1) - 1)
    def _():
        o_ref[...] = x_ref[...] + _rms(o_ref[...], gpost_ref[...])


def _mlp(x, g_pre, w_up_all, w_down_all, g_post, layer):
    m, d = x.shape
    ff = w_up_all.shape[2]
    tm = min(1024, m)
    tf = min(512, ff)
    return pl.pallas_call(
        _mlp_kernel,
        grid=(m // tm, ff // tf),
        in_specs=[
            pl.BlockSpec((tm, d), lambda i, j: (i, 0)),
            pl.BlockSpec((1, d), lambda i, j: (0, 0)),
            pl.BlockSpec((None, d, tf), lambda i, j: (layer, 0, j)),
            pl.BlockSpec((None, tf, d), lambda i, j: (layer, j, 0)),
            pl.BlockSpec((1, d), lambda i, j: (0, 0)),
        ],
        out_specs=pl.BlockSpec((tm, d), lambda i, j: (i, 0)),
        out_shape=jax.ShapeDtypeStruct((m, d), F32),
        scratch_shapes=[pltpu.VMEM((tm, d), BF16)],
        compiler_params=_params("parallel", "arbitrary", vmem=VMEM_LIMIT_MLP),
        name="relu2_mlp",
    )(x, g_pre, w_up_all, w_down_all, g_post)


def kernel(x, norm_mix_pre, norm_mix_post, norm_mlp_pre, norm_mlp_post, w_in, gm_v_gain, gm_ws, gm_bs,
           gm_out_gain, rk_mu, rk_w0, rk_w_up, rk_a0, rk_a_up, rk_g_up, rk_k_k, rk_k_a, rk_r_k, rk_ln_w,
           rk_ln_b, sc_conv, sc_out_gain, w_out, mlp_up, mlp_down):
    batch, seq_len, d = x.shape
    depth = w_in.shape[0]
    mix_a = gm_v_gain.shape[1]
    mix_b = rk_w0.shape[1]
    mix_c = sc_out_gain.shape[1]
    n_decay, n_iclr, n_gate = rk_w_up.shape[1], rk_a_up.shape[1], rk_g_up.shape[1]
    n_lora = n_decay + n_iclr
    lora_pad = -(-n_lora // LANES) * LANES
    col_b = 2 * mix_a
    col_lora = col_b + 3 * mix_b
    col_gate = col_lora + n_lora
    col_c = col_gate + n_gate
    row = lambda a: a.reshape(1, -1)

    xf = x.reshape(batch * seq_len, d)
    n_in = w_in.shape[2]
    n_whole = n_in // PROJ_TILE_N * PROJ_TILE_N
    assert n_whole < n_in
    w_main = w_in[:, :, :n_whole].astype(BF16)
    w_tail = jnp.pad(w_in[:, :, n_whole:], ((0, 0), (0, 0), (0, n_whole + PROJ_TILE_N - n_in))).astype(BF16)
    lora_blk = -(-(n_lora + n_gate) // LANES) * LANES
    conv_phase = col_c % LANES
    for l in range(depth):
        mu_b = jnp.pad(rk_mu[l], (0, 3 * mix_b + lora_blk - rk_mu.shape[1])).reshape(1, -1)
        wwa = jnp.zeros((lora_pad, 2 * mix_b), F32)
        wwa = wwa.at[:n_decay, :mix_b].set(rk_w_up[l]).at[n_decay:n_lora, mix_b:].set(rk_a_up[l]).astype(BF16)
        g_up = jnp.zeros((lora_blk, mix_b), F32).at[n_lora:n_lora + n_gate].set(rk_g_up[l]).astype(BF16)

        p = _norm_matmul(xf, row(norm_mix_pre[l]), w_main, w_tail, l)

        ya = _gmlp(p, 0, row(gm_v_gain[l]), gm_ws[l], gm_bs[l].T, row(gm_out_gain[l]))
        yb = _rwkv_mix(p, col_b, mu_b, wwa, g_up, row(rk_w0[l]), row(rk_a0[l]), row(rk_k_k[l]),
                       row(rk_k_a[l]), row(rk_ln_w[l]), row(rk_ln_b[l]), row(rk_r_k[l]), batch, seq_len, n_decay)
        yc = _short_conv(p, col_c, sc_conv[l], row(sc_out_gain[l]), seq_len)

        xf = _out_proj(ya, yb, yc, conv_phase, w_out, l, row(norm_mix_post[l]), xf)
        xf = _mlp(xf, row(norm_mlp_pre[l]), mlp_up, mlp_down, row(norm_mlp_post[l]), l)
    return xf.reshape(batch, seq_len, d)
```

```python
import functools

import jax
import jax.numpy as jnp
from jax import lax
from jax.experimental import pallas as pl
from jax.experimental.pallas import tpu as pltpu

F32 = jnp.float32
BF16 = jnp.bfloat16

RMS_EPS = 1e-6
LNX_EPS = 64e-5
RWKV_HEAD = 64
CONV_GROUP_W = 64
LANES = 128
SUBLANES = 8
WKV_STEP = 128
PROJ_TILE_N = 1024
WKV_TILE_GROUP = 8
V7X_VMEM_BYTES = 64 * 1024 * 1024
VMEM_LIMIT = 56 * 1024 * 1024
VMEM_LIMIT_MLP = V7X_VMEM_BYTES - 2 * 1024 * 1024


def _params(*sem, vmem=VMEM_LIMIT):
    return pltpu.CompilerParams(dimension_semantics=sem, vmem_limit_bytes=vmem)


def _rms(x, gain):
    return x * lax.rsqrt(jnp.mean(x * x, axis=-1, keepdims=True) + RMS_EPS) * gain


def _dot(a, b):
    return jnp.dot(a, b, preferred_element_type=F32)


def _dot_exact_lhs(a, b):
    b0 = b.astype(BF16)
    r1 = b - b0.astype(F32)
    b1 = r1.astype(BF16)
    b2 = (r1 - b1.astype(F32)).astype(BF16)
    ab = a.astype(BF16)
    return _dot(ab, b0) + (_dot(ab, b1) + _dot(ab, b2))


def _dot_nt(a, b):
    return lax.dot_general(a, b, (((1,), (1,)), ((), ())), preferred_element_type=F32)


def _dot_tn(a, b):
    return lax.dot_general(a, b, (((0,), (0,)), ((), ())), preferred_element_type=F32)


def _group_ones(width):
    i = lax.broadcasted_iota(jnp.int32, (LANES, LANES), 0) // width
    j = lax.broadcasted_iota(jnp.int32, (LANES, LANES), 1) // width
    return jnp.where(i == j, 1.0, 0.0).astype(BF16)


def _seg_sum(x, ones_bd):
    xb = x.astype(BF16)
    cols = [_dot(xb[:, c * LANES:(c + 1) * LANES], ones_bd) for c in range(x.shape[-1] // LANES)]
    return cols[0] if len(cols) == 1 else jnp.concatenate(cols, axis=-1)


def _shift_rows(x, prev_rows, n):
    rolled = pltpu.roll(x, n, axis=0)
    row = lax.broadcasted_iota(jnp.int32, x.shape, 0)
    p = prev_rows.shape[0]
    for j in range(n):
        rolled = jnp.where(row == j, prev_rows[p - n + j:p - n + j + 1, :], rolled)
    return rolled


def _norm_matmul_kernel(x_ref, g_ref, w_ref, o_ref, h_ref):
    @pl.when(pl.program_id(1) == 0)
    def _():
        h_ref[...] = _rms(x_ref[...], g_ref[...]).astype(BF16)

    o_ref[...] = _dot(h_ref[...], w_ref[...])


def _norm_matmul(x, gain, w_all, layer):
    m, d = x.shape
    n = w_all.shape[2]
    tm = min(1024, m)
    tn = PROJ_TILE_N
    return pl.pallas_call(
        _norm_matmul_kernel,
        grid=(m // tm, n // tn),
        in_specs=[
            pl.BlockSpec((tm, d), lambda i, j: (i, 0)),
            pl.BlockSpec((1, d), lambda i, j: (0, 0)),
            pl.BlockSpec((None, d, tn), lambda i, j: (layer, 0, j)),
        ],
        out_specs=pl.BlockSpec((tm, tn), lambda i, j: (i, j)),
        out_shape=jax.ShapeDtypeStruct((m, n), F32),
        scratch_shapes=[pltpu.VMEM((tm, d), BF16)],
        compiler_params=_params("parallel", "arbitrary"),
        name="in_proj",
    )(x, gain, w_all)


def _gelu_tanh(x):
    return 0.5 * x * (1.0 + jnp.tanh(0.7978845608028654 * (x + 0.044715 * (x * x * x))))


def _gmlp_kernel(pu_ref, pv_ref, vg_ref, ws_ref, bst_ref, og_ref, o_ref):
    groups, chunk, _ = ws_ref.shape
    tm = pu_ref.shape[0]
    zu = _gelu_tanh(pu_ref[...])
    zv = _gelu_tanh(pv_ref[...])
    row = lax.broadcasted_iota(jnp.int32, (chunk, chunk), 0)
    col = lax.broadcasted_iota(jnp.int32, (chunk, chunk), 1)
    bst = bst_ref[...]
    for g in range(groups):
        sl = slice(g * LANES, (g + 1) * LANES)
        u = zu[:, sl]
        v = _rms(zv[:, sl], vg_ref[:, sl]).astype(BF16)
        w = jnp.where(col <= row, ws_ref[g], 0.0).astype(BF16)
        bias = bst[:, g:g + 1]
        for c in range(tm // chunk):
            rows = slice(c * chunk, (c + 1) * chunk)
            mixed = _dot(w, v[rows]) + bias
            o_ref[rows, sl] = _rms(u[rows] * mixed, og_ref[:, sl])


def _gmlp(p, col0, v_gain, ws, bs_t, out_gain):
    m = p.shape[0]
    groups, chunk, _ = ws.shape
    half = v_gain.shape[1]
    cb = col0 // half
    assert cb * half == col0
    tm = min(512, m)
    return pl.pallas_call(
        _gmlp_kernel,
        grid=(m // tm,),
        in_specs=[
            pl.BlockSpec((tm, half), lambda i: (i, cb)),
            pl.BlockSpec((tm, half), lambda i: (i, cb + 1)),
            pl.BlockSpec((1, half), lambda i: (0, 0)),
            pl.BlockSpec((groups, chunk, chunk), lambda i: (0, 0, 0)),
            pl.BlockSpec((chunk, groups), lambda i: (0, 0)),
            pl.BlockSpec((1, half), lambda i: (0, 0)),
        ],
        out_specs=pl.BlockSpec((tm, half), lambda i: (i, 0)),
        out_shape=jax.ShapeDtypeStruct((m, half), F32),
        compiler_params=_params("parallel"),
        name="gmlp_mix",
    )(p, p, v_gain, ws, bs_t, out_gain)


def _conv_kernel(seq_len, starts, p_ref, prev_ref, cw_ref, og_ref, o_ref):
    tm = p_ref.shape[0]
    fw = o_ref.shape[1]
    sg, sc, sh = starts
    p = p_ref[...]
    gb, z = p[:, sg:sg + fw], p[:, sc:sc + fw] * p[:, sh:sh + fw]
    pv = prev_ref[...]
    seq_start = (pl.program_id(0) * tm) % seq_len == 0
    zp = jnp.where(seq_start, 0.0, pv[:, sc:sc + fw] * pv[:, sh:sh + fw])
    cw = cw_ref[...]
    taps = cw.shape[0]
    zc = cw[taps - 1:taps, :] * z
    for s in range(1, taps):
        zc = zc + cw[taps - 1 - s:taps - s, :] * _shift_rows(z, zp, s)
    y = gb * zc
    ms = _seg_sum(y * y, _group_ones(CONV_GROUP_W)) * (1.0 / CONV_GROUP_W)
    o_ref[...] = y * lax.rsqrt(ms + RMS_EPS) * og_ref[...]


def _short_conv(p, col0, conv_w, out_gain, seq_len):
    m, n_all = p.shape
    c = out_gain.shape[1]
    phase = col0 % LANES
    fw = c + LANES
    win = 4 * c
    wb = col0 // win
    assert (wb + 1) * win <= n_all and c % LANES == 0 and phase % CONV_GROUP_W == 0
    starts = tuple(col0 + j * c - phase - wb * win for j in range(3))
    assert starts[0] >= 0 and starts[2] + fw <= win
    pad = lambda a: jnp.pad(a, ((0, 0), (phase, LANES - phase)))
    tm = min(512, seq_len)
    assert seq_len % tm == 0, "a row block must not straddle two sequences"
    rb = tm // SUBLANES
    return pl.pallas_call(
        functools.partial(_conv_kernel, seq_len, starts),
        grid=(m // tm,),
        in_specs=[
            pl.BlockSpec((tm, win), lambda i: (i, wb)),
            pl.BlockSpec((SUBLANES, win), lambda i: (jnp.maximum(i * rb - 1, 0), wb)),
            pl.BlockSpec((conv_w.shape[0], fw), lambda i: (0, 0)),
            pl.BlockSpec((1, fw), lambda i: (0, 0)),
        ],
        out_specs=pl.BlockSpec((tm, fw), lambda i: (i, 0)),
        out_shape=jax.ShapeDtypeStruct((m, fw), F32),
        compiler_params=_params("parallel"),
        name="short_conv_mix",
    )(p, p, pad(conv_w), pad(out_gain))


def _rwkv_operands(p, prev_row, mu, wwa, g_up, w0, a0, k_k, k_a, n_decay):
    c = w0.shape[1]
    ps = p + (_shift_rows(p, prev_row, 1) - p) * mu
    r, k, v = ps[:, :c], ps[:, c:2 * c], ps[:, 2 * c:3 * c]
    lo = ps[:, 3 * c:3 * c + wwa.shape[0]]
    gd = ps[:, 3 * c:]
    lane = lax.broadcasted_iota(jnp.int32, lo.shape, 1)
    act = jnp.where(lane < n_decay, jnp.tanh(lo), lo)
    wa = _dot(act.astype(BF16), wwa)
    zw = w0 + wa[:, :c]
    w = -(jnp.maximum(-zw, 0.0) + jnp.log(1.0 + jnp.exp(-jnp.abs(zw)))) - 0.5
    iclr = 1.0 / (1.0 + jnp.exp(-(a0 + wa[:, c:])))
    g = _dot((1.0 / (1.0 + jnp.exp(-gd))).astype(BF16), g_up)
    kkf = k * k_k
    nrm = jnp.sqrt(_seg_sum(kkf * kkf, _group_ones(RWKV_HEAD)))
    kk = kkf / jnp.maximum(nrm, 1e-12)
    return r, k * (1.0 + (iclr - 1.0) * k_a), v, -jnp.exp(w), -kk, kk * iclr, g


def _wkv_kernel(n_decay, pr_ref, pk_ref, pv_ref, pl_ref, mu_ref, wwa_ref, gup_ref, w0_ref, a0_ref, kk_ref, ka_ref,
                lnw_ref, lnb_ref, rk_ref, o_ref, h_ref, prev_ref):
    @pl.when(pl.program_id(1) == 0)
    def _():
        h_ref[...] = jnp.zeros_like(h_ref)
        prev_ref[...] = jnp.zeros_like(prev_ref)

    TB, C = o_ref.shape
    L = RWKV_HEAD
    NCH = TB // L
    HPT = LANES // RWKV_HEAD
    CPP = 2
    NB = CPP * HPT
    p = jnp.concatenate([pr_ref[...], pk_ref[...], pv_ref[...], pl_ref[...]], axis=1)
    r, k, v, lw, a, b, g = _rwkv_operands(p, prev_ref[0:1, :], mu_ref[...], wwa_ref[...], gup_ref[...], w0_ref[...],
                                          a0_ref[...], kk_ref[...], ka_ref[...], n_decay)
    prev_ref[0:1, :] = p[TB - 1:TB, :]
    row = lax.broadcasted_iota(jnp.int32, (TB, TB), 0)
    col = lax.broadcasted_iota(jnp.int32, (TB, TB), 1)
    cum = _dot_exact_lhs(jnp.where((col <= row) & (row // L == col // L), 1.0, 0.0), lw)

    def chunk_rows(i):
        return jnp.concatenate([jnp.broadcast_to(cum[c * L + i:c * L + i + 1], (L, C)) for c in range(NCH)], axis=0)

    c_mid = chunk_rows(L // 2 - 1)
    c_end = chunk_rows(L - 1)
    e_fwd = jnp.exp(cum - c_mid)
    e_bwd = jnp.exp(c_mid - cum)
    e_end = jnp.exp(c_end - cum)
    bf = lambda z: z.astype(BF16)
    rt = bf(r * e_fwd)
    at = bf(a * jnp.exp(cum - lw - c_mid))
    bt = bf(b * e_bwd)
    kt = bf(k * e_bwd)
    bh = bf(b * e_end)
    kh = bf(k * e_end)
    vb = bf(v)

    lane = lax.broadcasted_iota(jnp.int32, (1, LANES), 1)
    hi = lax.broadcasted_iota(jnp.int32, (LANES, LANES), 0) // RWKV_HEAD
    hj = lax.broadcasted_iota(jnp.int32, (LANES, LANES), 1) // RWKV_HEAD
    same_head = hi == hj
    zero_b = jnp.zeros((), BF16)
    zeros_lt = jnp.zeros((L, LANES), BF16)

    chunks = range(NCH)
    rows = [slice(c * L, (c + 1) * L) for c in chunks]
    head_mask = [(lane // RWKV_HEAD) == h for h in range(HPT)]
    head_mask2 = [jnp.concatenate([m, m], axis=1) for m in head_mask]
    low_half = lane < RWKV_HEAD
    lr = lax.broadcasted_iota(jnp.int32, (L, LANES), 1) % L
    rr = lax.broadcasted_iota(jnp.int32, (L, LANES), 0)
    strict = lr < rr
    lower = lr <= rr
    lrp = lax.broadcasted_iota(jnp.int32, (L, NB * L), 1) % L
    rrp = lax.broadcasted_iota(jnp.int32, (L, NB * L), 0)
    blkp = lax.broadcasted_iota(jnp.int32, (L, NB * L), 1) // L
    same_blk = lambda size: (lrp // size) == (rrp // size)

    def block_diag(xp):
        return jnp.concatenate([jnp.where(blkp == q, xp, zero_b) for q in range(NB)], axis=0)

    def run_tiles(t0, nt):
        tiles = range(nt)
        tile_sl = [slice((t0 + t) * LANES, (t0 + t + 1) * LANES) for t in tiles]
        ppt = NCH // CPP
        units = [(t, p) for t in tiles for p in range(ppt)]
        uidx = range(len(units))
        lch = range(CPP)
        usl = lambda u: tile_sl[units[u][0]]
        urows = lambda u, c: rows[units[u][1] * CPP + c]

        def masked(x, u, c, h):
            return jnp.where(head_mask[h], x[urows(u, c), usl(u)], zero_b)

        am = [[[masked(at, u, c, h) for h in range(HPT)] for c in lch] for u in uidx]
        vm = [[[masked(vb, u, c, h) for h in range(HPT)] for c in lch] for u in uidx]

        def score(u, c, h):
            b_, k_ = bt[urows(u, c), usl(u)], kt[urows(u, c), usl(u)]
            rhs = jnp.concatenate([b_, k_] if h == 0 else [k_, b_], axis=0)
            return _dot_nt(jnp.concatenate([am[u][c][h], masked(rt, u, c, h)], axis=0), rhs)

        s = [[[score(u, c, h) for h in range(HPT)] for c in lch] for u in uidx]

        def pick(u, c, part, first):
            e, o = s[u][c][0][part * L:(part + 1) * L], s[u][c][1][part * L:(part + 1) * L]
            return jnp.where(low_half, e, o) if first == 0 else jnp.where(low_half, o, e)

        n_p = [jnp.concatenate([jnp.where(strict, pick(u, c, 0, 0), 0.0) for c in lch], axis=1) for u in uidx]
        ak_p = [jnp.concatenate([bf(jnp.where(strict, pick(u, c, 0, 1), 0.0)) for c in lch], axis=1) for u in uidx]
        rb = [[bf(jnp.where(lower, pick(u, c, 1, 0), 0.0)) for c in lch] for u in uidx]
        rk = [[bf(jnp.where(lower, pick(u, c, 1, 1), 0.0)) for c in lch] for u in uidx]
        eye_p = jnp.where(lrp == rrp, 1.0, 0.0)
        tinv = [eye_p + jnp.where(same_blk(2), n, 0.0) for n in n_p]
        size = 2
        while size < L:
            sub_diag = same_blk(2 * size) & jnp.logical_not(same_blk(size))
            e_bd = [block_diag(bf(jnp.where(sub_diag, n, 0.0))) for n in n_p]
            tb = [bf(ti) for ti in tinv]
            te = [bf(_dot(tb_, e_)) for tb_, e_ in zip(tb, e_bd)]
            tinv = [ti + _dot(te_, block_diag(tb_)) for ti, te_, tb_ in zip(tinv, te, tb)]
            size *= 2
        tb = [bf(ti) for ti in tinv]

        def place(x, c):
            return jnp.concatenate([x if cc == c else zeros_lt for cc in lch], axis=1)

        w = [_dot(ak_p[u], jnp.concatenate([place(vm[u][c][h], c) for c in lch for h in (1, 0)], axis=0))
             for u in uidx]

        def solve(u, c):
            wc = bf(w[u][:, c * LANES:(c + 1) * LANES])
            rhs = jnp.concatenate([jnp.concatenate([am[u][c][h], jnp.where(head_mask[h], wc, zero_b)], axis=1)
                                   for h in range(HPT)], axis=0)
            return _dot(tb[u][:, c * LANES:(c + 1) * LANES], rhs)

        xs = [[solve(u, c) for c in lch] for u in uidx]

        def readout(u, c):
            xb = bf(xs[u][c])
            rhs = jnp.concatenate([jnp.where(head_mask2[0], xb, zero_b), jnp.where(head_mask2[1], xb, zero_b),
                                   jnp.concatenate([zeros_lt, vm[u][c][1]], axis=1),
                                   jnp.concatenate([zeros_lt, vm[u][c][0]], axis=1)], axis=0)
            return _dot(jnp.concatenate([rb[u][c], rk[u][c]], axis=1), rhs)

        zs = [[readout(u, c) for c in lch] for u in uidx]
        x = [[xs[t * ppt + c // CPP][c % CPP] for c in chunks] for t in tiles]
        z = [[zs[t * ppt + c // CPP][c % CPP] for c in chunks] for t in tiles]

        h = [h_ref[t0 + t] for t in tiles]
        y_rows = [[None] * NCH for _ in tiles]
        for c in chunks:
            dec_mid = jnp.exp(cum[c * L + L // 2 - 1:c * L + L // 2, :])
            dec = jnp.exp(cum[c * L + L - 1:c * L + L, :])
            h_mid = [bf(h[t] * jnp.transpose(jnp.broadcast_to(dec_mid[:, tile_sl[t]], (LANES, LANES)))) for t in tiles]
            lhs = [bf(jnp.concatenate([x[t][c][:, :LANES],
                                       rt[rows[c], tile_sl[t]].astype(F32) + z[t][c][:, :LANES]], axis=0)) for t in tiles]
            ah = [_dot(lhs[t], h_mid[t]) for t in tiles]
            u = [ah[t][:L] + x[t][c][:, LANES:] for t in tiles]
            for t in tiles:
                y_rows[t][c] = ah[t][L:] + z[t][c][:, LANES:]
            upd = [_dot_tn(jnp.concatenate([bh[rows[c], tile_sl[t]], kh[rows[c], tile_sl[t]]], axis=0),
                           jnp.concatenate([bf(u[t]), vb[rows[c], tile_sl[t]]], axis=0)) for t in tiles]
            h = [h[t] * jnp.transpose(jnp.broadcast_to(dec[:, tile_sl[t]], (LANES, LANES)))
                 + jnp.where(same_head, upd[t], 0.0) for t in tiles]
        for t in tiles:
            h_ref[t0 + t] = h[t]
        return [jnp.concatenate(y_rows[t], axis=0) for t in tiles]

    n_tiles = C // LANES
    y_tiles = []
    for t0 in range(0, n_tiles, WKV_TILE_GROUP):
        y_tiles += run_tiles(t0, min(WKV_TILE_GROUP, n_tiles - t0))
    y = jnp.concatenate(y_tiles, axis=1)
    ones_bd = _group_ones(RWKV_HEAD)
    inv_n = 1.0 / RWKV_HEAD
    mean = _seg_sum(y, ones_bd) * inv_n
    d = y - mean
    var = _seg_sum(d * d, ones_bd) * inv_n
    yn = d * lax.rsqrt(var + LNX_EPS) * lnw_ref[...] + lnb_ref[...]
    bonus = _seg_sum(r * k * rk_ref[...], ones_bd) * v
    o_ref[...] = (yn + bonus) * g


def _rwkv_mix(p, col0, mu, wwa, g_up, w0, a0, k_k, k_a, ln_w, ln_b, r_k, batch, seq_len, n_decay):
    m = p.shape[0]
    nb = mu.shape[1]
    c = w0.shape[1]
    lw_ = g_up.shape[0]
    assert col0 % c == 0 and (col0 + 3 * c) % lw_ == 0 and nb == 3 * c + lw_
    cb, lb = col0 // c, (col0 + 3 * c) // lw_
    tb = min(WKV_STEP, seq_len)
    assert LANES == 2 * RWKV_HEAD and RWKV_HEAD & (RWKV_HEAD - 1) == 0, "kernel packs two power-of-two heads per lane tile"
    assert tb % (2 * RWKV_HEAD) == 0 and seq_len % tb == 0 and c % LANES == 0
    nc = seq_len // tb
    vec = lambda n: pl.BlockSpec((1, n), lambda bi, ci: (0, 0))
    full = lambda a: pl.BlockSpec(a.shape, lambda bi, ci: (0, 0))
    return pl.pallas_call(
        functools.partial(_wkv_kernel, n_decay),
        grid=(batch, nc),
        in_specs=[pl.BlockSpec((tb, c), lambda bi, ci: (bi * nc + ci, cb)),
                  pl.BlockSpec((tb, c), lambda bi, ci: (bi * nc + ci, cb + 1)),
                  pl.BlockSpec((tb, c), lambda bi, ci: (bi * nc + ci, cb + 2)),
                  pl.BlockSpec((tb, lw_), lambda bi, ci: (bi * nc + ci, lb)),
                  vec(nb), full(wwa), full(g_up)] + [vec(c)] * 7,
        out_specs=pl.BlockSpec((tb, c), lambda bi, ci: (bi * nc + ci, 0)),
        out_shape=jax.ShapeDtypeStruct((m, c), F32),
        scratch_shapes=[pltpu.VMEM((c // LANES, LANES, LANES), F32), pltpu.VMEM((SUBLANES, nb), F32)],
        compiler_params=_params("parallel", "arbitrary"),
        name="rwkv7_mix",
    )(p, p, p, p, mu, wwa, g_up, w0, a0, k_k, k_a, ln_w, ln_b, r_k)


def _out_proj_kernel(phase, ya_ref, yb_ref, yc_ref, w_ref, g_ref, x_ref, o_ref, wb_ref):
    @pl.when(pl.program_id(0) == 0)
    def _():
        wb_ref[...] = w_ref[...].astype(BF16)

    ra = ya_ref.shape[1]
    rc = ra + yb_ref.shape[1]
    rows, d = wb_ref.shape
    start = rc - phase
    full = (rows - start) // LANES * LANES
    yc = yc_ref[...].astype(BF16)
    w_tail = jnp.concatenate([wb_ref[start + full:rows, :],
                              jnp.zeros((LANES - (rows - start - full), d), BF16)], axis=0)
    y = (_dot(ya_ref[...].astype(BF16), wb_ref[0:ra, :]) + _dot(yb_ref[...].astype(BF16), wb_ref[ra:rc, :])
         + _dot(yc[:, :full], wb_ref[start:start + full, :]) + _dot(yc[:, full:], w_tail))
    o_ref[...] = x_ref[...] + _rms(y, g_ref[...])


def _out_proj(ya, yb, yc, phase, w_all, layer, gain, x):
    m, d = x.shape
    rows = w_all.shape[1]
    tm = min(256, m)
    start = ya.shape[1] + yb.shape[1] - phase
    assert yc.shape[1] == (rows - start) // LANES * LANES + LANES and 0 < (rows - start) % LANES
    act = lambda a: pl.BlockSpec((tm, a.shape[1]), lambda i: (i, 0))
    return pl.pallas_call(
        functools.partial(_out_proj_kernel, phase),
        grid=(m // tm,),
        in_specs=[act(ya), act(yb), act(yc),
                  pl.BlockSpec((None, rows, d), lambda i: (layer, 0, 0), pipeline_mode=pl.Buffered(1)),
                  pl.BlockSpec((1, d), lambda i: (0, 0)), act(x)],
        out_specs=act(x),
        out_shape=jax.ShapeDtypeStruct((m, d), F32),
        scratch_shapes=[pltpu.VMEM((rows, d), BF16)],
        compiler_params=_params("arbitrary"),
        name="out_proj",
    )(ya, yb, yc, w_all, gain, x)


def _mlp_kernel(x_ref, gpre_ref, wu_ref, wd_ref, gpost_ref, o_ref, h_ref):
    j = pl.program_id(1)

    @pl.when(j == 0)
    def _():
        h_ref[...] = _rms(x_ref[...], gpre_ref[...]).astype(BF16)
        o_ref[...] = jnp.zeros_like(o_ref)

    f = jnp.maximum(_dot(h_ref[...], wu_ref[...].astype(BF16)), 0.0)
    o_ref[...] += _dot((f * f).astype(BF16), wd_ref[...].astype(BF16))

    @pl.when(j == pl.num_programs(1) - 1)
    def _():
        o_ref[...] = x_ref[...] + _rms(o_ref[...], gpost_ref[...])


def _mlp(x, g_pre, w_up_all, w_down_all, g_post, layer):
    m, d = x.shape
    ff = w_up_all.shape[2]
    tm = min(1024, m)
    tf = min(512, ff)
    return pl.pallas_call(
        _mlp_kernel,
        grid=(m // tm, ff // tf),
        in_specs=[
            pl.BlockSpec((tm, d), lambda i, j: (i, 0)),
            pl.BlockSpec((1, d), lambda i, j: (0, 0)),
            pl.BlockSpec((None, d, tf), lambda i, j: (layer, 0, j)),
            pl.BlockSpec((None, tf, d), lambda i, j: (layer, j, 0)),
            pl.BlockSpec((1, d), lambda i, j: (0, 0)),
        ],
        out_specs=pl.BlockSpec((tm, d), lambda i, j: (i, 0)),
        out_shape=jax.ShapeDtypeStruct((m, d), F32),
        scratch_shapes=[pltpu.VMEM((tm, d), BF16)],
        compiler_params=_params("parallel", "arbitrary", vmem=VMEM_LIMIT_MLP),
        name="relu2_mlp",
    )(x, g_pre, w_up_all, w_down_all, g_post)


def kernel(x, norm_mix_pre, norm_mix_post, norm_mlp_pre, norm_mlp_post, w_in, gm_v_gain, gm_ws, gm_bs,
           gm_out_gain, rk_mu, rk_w0, rk_w_up, rk_a0, rk_a_up, rk_g_up, rk_k_k, rk_k_a, rk_r_k, rk_ln_w,
           rk_ln_b, sc_conv, sc_out_gain, w_out, mlp_up, mlp_down):
    batch, seq_len, d = x.shape
    depth = w_in.shape[0]
    mix_a = gm_v_gain.shape[1]
    mix_b = rk_w0.shape[1]
    mix_c = sc_out_gain.shape[1]
    n_decay, n_iclr, n_gate = rk_w_up.shape[1], rk_a_up.shape[1], rk_g_up.shape[1]
    n_lora = n_decay + n_iclr
    lora_pad = -(-n_lora // LANES) * LANES
    col_b = 2 * mix_a
    col_lora = col_b + 3 * mix_b
    col_gate = col_lora + n_lora
    col_c = col_gate + n_gate
    row = lambda a: a.reshape(1, -1)

    xf = x.reshape(batch * seq_len, d)
    n_in = w_in.shape[2]
    n_proj = -(-n_in // PROJ_TILE_N) * PROJ_TILE_N
    w_proj = jnp.concatenate([w_in.astype(BF16), jnp.zeros((depth, d, n_proj - n_in), BF16)], axis=2)
    lora_blk = -(-(n_lora + n_gate) // LANES) * LANES
    conv_phase = col_c % LANES
    for l in range(depth):
        mu_b = jnp.pad(rk_mu[l], (0, 3 * mix_b + lora_blk - rk_mu.shape[1])).reshape(1, -1)
        wwa = jnp.zeros((lora_pad, 2 * mix_b), F32)
        wwa = wwa.at[:n_decay, :mix_b].set(rk_w_up[l]).at[n_decay:n_lora, mix_b:].set(rk_a_up[l]).astype(BF16)
        g_up = jnp.zeros((lora_blk, mix_b), F32).at[n_lora:n_lora + n_gate].set(rk_g_up[l]).astype(BF16)

        p = _norm_matmul(xf, row(norm_mix_pre[l]), w_proj, l)

        ya = _gmlp(p, 0, row(gm_v_gain[l]), gm_ws[l], gm_bs[l].T, row(gm_out_gain[l]))
        yb = _rwkv_mix(p, col_b, mu_b, wwa, g_up, row(rk_w0[l]), row(rk_a0[l]), row(rk_k_k[l]),
                       row(rk_k_a[l]), row(rk_ln_w[l]), row(rk_ln_b[l]), row(rk_r_k[l]), batch, seq_len, n_decay)
        yc = _short_conv(p, col_c, sc_conv[l], row(sc_out_gain[l]), seq_len)

        xf = _out_proj(ya, yb, yc, conv_phase, w_out, l, row(norm_mix_post[l]), xf)
        xf = _mlp(xf, row(norm_mlp_pre[l]), mlp_up, mlp_down, row(norm_mlp_post[l]), l)
    return xf.reshape(batch, seq_len, d)
```

```python
import functools

import jax
import jax.numpy as jnp
from jax import lax
from jax.experimental import pallas as pl
from jax.experimental.pallas import tpu as pltpu

F32 = jnp.float32
BF16 = jnp.bfloat16

RMS_EPS = 1e-6
LNX_EPS = 64e-5
RWKV_HEAD = 64
CONV_GROUP_W = 64
LANES = 128
SUBLANES = 8
WKV_STEP = 128
PROJ_TILE_N = 1024
WKV_TILE_GROUP = 8
V7X_VMEM_BYTES = 64 * 1024 * 1024
VMEM_LIMIT = 56 * 1024 * 1024
VMEM_LIMIT_MLP = V7X_VMEM_BYTES - 2 * 1024 * 1024


def _params(*sem, vmem=VMEM_LIMIT):
    return pltpu.CompilerParams(dimension_semantics=sem, vmem_limit_bytes=vmem)


def _rms(x, gain):
    return x * lax.rsqrt(jnp.mean(x * x, axis=-1, keepdims=True) + RMS_EPS) * gain


def _dot(a, b):
    return jnp.dot(a, b, preferred_element_type=F32)


def _dot_exact_lhs(a, b):
    b0 = b.astype(BF16)
    r1 = b - b0.astype(F32)
    b1 = r1.astype(BF16)
    b2 = (r1 - b1.astype(F32)).astype(BF16)
    ab = a.astype(BF16)
    return _dot(ab, b0) + (_dot(ab, b1) + _dot(ab, b2))


def _dot_nt(a, b):
    return lax.dot_general(a, b, (((1,), (1,)), ((), ())), preferred_element_type=F32)


def _dot_tn(a, b):
    return lax.dot_general(a, b, (((0,), (0,)), ((), ())), preferred_element_type=F32)


def _group_ones(width):
    i = lax.broadcasted_iota(jnp.int32, (LANES, LANES), 0) // width
    j = lax.broadcasted_iota(jnp.int32, (LANES, LANES), 1) // width
    return jnp.where(i == j, 1.0, 0.0).astype(BF16)


def _seg_sum(x, ones_bd):
    xb = x.astype(BF16)
    cols = [_dot(xb[:, c * LANES:(c + 1) * LANES], ones_bd) for c in range(x.shape[-1] // LANES)]
    return cols[0] if len(cols) == 1 else jnp.concatenate(cols, axis=-1)


def _shift_rows(x, prev_rows, n):
    rolled = pltpu.roll(x, n, axis=0)
    row = lax.broadcasted_iota(jnp.int32, x.shape, 0)
    p = prev_rows.shape[0]
    for j in range(n):
        rolled = jnp.where(row == j, prev_rows[p - n + j:p - n + j + 1, :], rolled)
    return rolled


def _norm_matmul_kernel(n_full, x_ref, g_ref, w_ref, wt_ref, o_ref, h_ref):
    j = pl.program_id(1)

    @pl.when(j == 0)
    def _():
        h_ref[...] = _rms(x_ref[...], g_ref[...]).astype(BF16)

    @pl.when(j < n_full)
    def _():
        o_ref[...] = _dot(h_ref[...], w_ref[...])

    @pl.when(j >= n_full)
    def _():
        o_ref[...] = _dot(h_ref[...], wt_ref[...])


def _norm_matmul(x, gain, w_all, w_tail, layer):
    m, d = x.shape
    tm = min(1024, m)
    tn = w_tail.shape[2]
    n_full = w_all.shape[2] // tn
    return pl.pallas_call(
        functools.partial(_norm_matmul_kernel, n_full),
        grid=(m // tm, n_full + 1),
        in_specs=[
            pl.BlockSpec((tm, d), lambda i, j: (i, 0)),
            pl.BlockSpec((1, d), lambda i, j: (0, 0)),
            pl.BlockSpec((None, d, tn), lambda i, j: (layer, 0, jnp.minimum(j, n_full - 1))),
            pl.BlockSpec((None, d, tn), lambda i, j: (layer, 0, 0), pipeline_mode=pl.Buffered(1)),
        ],
        out_specs=pl.BlockSpec((tm, tn), lambda i, j: (i, j)),
        out_shape=jax.ShapeDtypeStruct((m, (n_full + 1) * tn), F32),
        scratch_shapes=[pltpu.VMEM((tm, d), BF16)],
        compiler_params=_params("parallel", "arbitrary"),
        name="in_proj",
    )(x, gain, w_all, w_tail)


def _gelu_tanh(x):
    return 0.5 * x * (1.0 + jnp.tanh(0.7978845608028654 * (x + 0.044715 * (x * x * x))))


def _gmlp_kernel(pu_ref, pv_ref, vg_ref, ws_ref, bst_ref, og_ref, o_ref):
    groups, chunk, _ = ws_ref.shape
    tm = pu_ref.shape[0]
    zu = _gelu_tanh(pu_ref[...])
    zv = _gelu_tanh(pv_ref[...])
    row = lax.broadcasted_iota(jnp.int32, (chunk, chunk), 0)
    col = lax.broadcasted_iota(jnp.int32, (chunk, chunk), 1)
    bst = bst_ref[...]
    for g in range(groups):
        sl = slice(g * LANES, (g + 1) * LANES)
        u = zu[:, sl]
        v = _rms(zv[:, sl], vg_ref[:, sl]).astype(BF16)
        w = jnp.where(col <= row, ws_ref[g], 0.0).astype(BF16)
        bias = bst[:, g:g + 1]
        for c in range(tm // chunk):
            rows = slice(c * chunk, (c + 1) * chunk)
            mixed = _dot(w, v[rows]) + bias
            o_ref[rows, sl] = _rms(u[rows] * mixed, og_ref[:, sl])


def _gmlp(p, col0, v_gain, ws, bs_t, out_gain):
    m = p.shape[0]
    groups, chunk, _ = ws.shape
    half = v_gain.shape[1]
    cb = col0 // half
    assert cb * half == col0
    tm = min(512, m)
    return pl.pallas_call(
        _gmlp_kernel,
        grid=(m // tm,),
        in_specs=[
            pl.BlockSpec((tm, half), lambda i: (i, cb)),
            pl.BlockSpec((tm, half), lambda i: (i, cb + 1)),
            pl.BlockSpec((1, half), lambda i: (0, 0)),
            pl.BlockSpec((groups, chunk, chunk), lambda i: (0, 0, 0)),
            pl.BlockSpec((chunk, groups), lambda i: (0, 0)),
            pl.BlockSpec((1, half), lambda i: (0, 0)),
        ],
        out_specs=pl.BlockSpec((tm, half), lambda i: (i, 0)),
        out_shape=jax.ShapeDtypeStruct((m, half), F32),
        compiler_params=_params("parallel"),
        name="gmlp_mix",
    )(p, p, v_gain, ws, bs_t, out_gain)


def _conv_kernel(seq_len, starts, p_ref, prev_ref, cw_ref, og_ref, o_ref):
    tm = p_ref.shape[0]
    fw = o_ref.shape[1]
    sg, sc, sh = starts
    p = p_ref[...]
    gb, z = p[:, sg:sg + fw], p[:, sc:sc + fw] * p[:, sh:sh + fw]
    pv = prev_ref[...]
    seq_start = (pl.program_id(0) * tm) % seq_len == 0
    zp = jnp.where(seq_start, 0.0, pv[:, sc:sc + fw] * pv[:, sh:sh + fw])
    cw = cw_ref[...]
    taps = cw.shape[0]
    zc = cw[taps - 1:taps, :] * z
    for s in range(1, taps):
        zc = zc + cw[taps - 1 - s:taps - s, :] * _shift_rows(z, zp, s)
    y = gb * zc
    ms = _seg_sum(y * y, _group_ones(CONV_GROUP_W)) * (1.0 / CONV_GROUP_W)
    o_ref[...] = y * lax.rsqrt(ms + RMS_EPS) * og_ref[...]


def _short_conv(p, col0, conv_w, out_gain, seq_len):
    m, n_all = p.shape
    c = out_gain.shape[1]
    phase = col0 % LANES
    fw = c + LANES
    win = 4 * c
    wb = col0 // win
    assert (wb + 1) * win <= n_all and c % LANES == 0 and phase % CONV_GROUP_W == 0
    starts = tuple(col0 + j * c - phase - wb * win for j in range(3))
    assert starts[0] >= 0 and starts[2] + fw <= win
    pad = lambda a: jnp.pad(a, ((0, 0), (phase, LANES - phase)))
    tm = min(512, seq_len)
    assert seq_len % tm == 0, "a row block must not straddle two sequences"
    rb = tm // SUBLANES
    return pl.pallas_call(
        functools.partial(_conv_kernel, seq_len, starts),
        grid=(m // tm,),
        in_specs=[
            pl.BlockSpec((tm, win), lambda i: (i, wb)),
            pl.BlockSpec((SUBLANES, win), lambda i: (jnp.maximum(i * rb - 1, 0), wb)),
            pl.BlockSpec((conv_w.shape[0], fw), lambda i: (0, 0)),
            pl.BlockSpec((1, fw), lambda i: (0, 0)),
        ],
        out_specs=pl.BlockSpec((tm, fw), lambda i: (i, 0)),
        out_shape=jax.ShapeDtypeStruct((m, fw), F32),
        compiler_params=_params("parallel"),
        name="short_conv_mix",
    )(p, p, pad(conv_w), pad(out_gain))


def _rwkv_operands(p, prev_row, mu, wwa, g_up, w0, a0, k_k, k_a, n_decay):
    c = w0.shape[1]
    ps = p + (_shift_rows(p, prev_row, 1) - p) * mu
    r, k, v = ps[:, :c], ps[:, c:2 * c], ps[:, 2 * c:3 * c]
    lo = ps[:, 3 * c:3 * c + wwa.shape[0]]
    gd = ps[:, 3 * c:]
    lane = lax.broadcasted_iota(jnp.int32, lo.shape, 1)
    act = jnp.where(lane < n_decay, jnp.tanh(lo), lo)
    wa = _dot(act.astype(BF16), wwa)
    zw = w0 + wa[:, :c]
    w = -(jnp.maximum(-zw, 0.0) + jnp.log(1.0 + jnp.exp(-jnp.abs(zw)))) - 0.5
    iclr = 1.0 / (1.0 + jnp.exp(-(a0 + wa[:, c:])))
    g = _dot((1.0 / (1.0 + jnp.exp(-gd))).astype(BF16), g_up)
    kkf = k * k_k
    nrm = jnp.sqrt(_seg_sum(kkf * kkf, _group_ones(RWKV_HEAD)))
    kk = kkf / jnp.maximum(nrm, 1e-12)
    return r, k * (1.0 + (iclr - 1.0) * k_a), v, -jnp.exp(w), -kk, kk * iclr, g


def _wkv_kernel(n_decay, pr_ref, pk_ref, pv_ref, pl_ref, mu_ref, wwa_ref, gup_ref, w0_ref, a0_ref, kk_ref, ka_ref,
                lnw_ref, lnb_ref, rk_ref, o_ref, h_ref, prev_ref):
    @pl.when(pl.program_id(1) == 0)
    def _():
        h_ref[...] = jnp.zeros_like(h_ref)
        prev_ref[...] = jnp.zeros_like(prev_ref)

    TB, C = o_ref.shape
    L = RWKV_HEAD
    NCH = TB // L
    HPT = LANES // RWKV_HEAD
    CPP = 2
    NB = CPP * HPT
    p = jnp.concatenate([pr_ref[...], pk_ref[...], pv_ref[...], pl_ref[...]], axis=1)
    r, k, v, lw, a, b, g = _rwkv_operands(p, prev_ref[0:1, :], mu_ref[...], wwa_ref[...], gup_ref[...], w0_ref[...],
                                          a0_ref[...], kk_ref[...], ka_ref[...], n_decay)
    prev_ref[0:1, :] = p[TB - 1:TB, :]
    row = lax.broadcasted_iota(jnp.int32, (TB, TB), 0)
    col = lax.broadcasted_iota(jnp.int32, (TB, TB), 1)
    cum = _dot_exact_lhs(jnp.where((col <= row) & (row // L == col // L), 1.0, 0.0), lw)

    def chunk_rows(i):
        return jnp.concatenate([jnp.broadcast_to(cum[c * L + i:c * L + i + 1], (L, C)) for c in range(NCH)], axis=0)

    c_mid = chunk_rows(L // 2 - 1)
    c_end = chunk_rows(L - 1)
    e_fwd = jnp.exp(cum - c_mid)
    e_bwd = jnp.exp(c_mid - cum)
    e_end = jnp.exp(c_end - cum)
    bf = lambda z: z.astype(BF16)
    rt = bf(r * e_fwd)
    at = bf(a * jnp.exp(cum - lw - c_mid))
    bt = bf(b * e_bwd)
    kt = bf(k * e_bwd)
    bh = bf(b * e_end)
    kh = bf(k * e_end)
    vb = bf(v)

    lane = lax.broadcasted_iota(jnp.int32, (1, LANES), 1)
    hi = lax.broadcasted_iota(jnp.int32, (LANES, LANES), 0) // RWKV_HEAD
    hj = lax.broadcasted_iota(jnp.int32, (LANES, LANES), 1) // RWKV_HEAD
    same_head = hi == hj
    zero_b = jnp.zeros((), BF16)
    zeros_lt = jnp.zeros((L, LANES), BF16)

    chunks = range(NCH)
    rows = [slice(c * L, (c + 1) * L) for c in chunks]
    head_mask = [(lane // RWKV_HEAD) == h for h in range(HPT)]
    head_mask2 = [jnp.concatenate([m, m], axis=1) for m in head_mask]
    low_half = lane < RWKV_HEAD
    lr = lax.broadcasted_iota(jnp.int32, (L, LANES), 1) % L
    rr = lax.broadcasted_iota(jnp.int32, (L, LANES), 0)
    strict = lr < rr
    lower = lr <= rr
    lrp = lax.broadcasted_iota(jnp.int32, (L, NB * L), 1) % L
    rrp = lax.broadcasted_iota(jnp.int32, (L, NB * L), 0)
    blkp = lax.broadcasted_iota(jnp.int32, (L, NB * L), 1) // L
    same_blk = lambda size: (lrp // size) == (rrp // size)

    def block_diag(xp):
        return jnp.concatenate([jnp.where(blkp == q, xp, zero_b) for q in range(NB)], axis=0)

    def run_tiles(t0, nt):
        tiles = range(nt)
        tile_sl = [slice((t0 + t) * LANES, (t0 + t + 1) * LANES) for t in tiles]
        ppt = NCH // CPP
        units = [(t, p) for t in tiles for p in range(ppt)]
        uidx = range(len(units))
        lch = range(CPP)
        usl = lambda u: tile_sl[units[u][0]]
        urows = lambda u, c: rows[units[u][1] * CPP + c]

        def masked(x, u, c, h):
            return jnp.where(head_mask[h], x[urows(u, c), usl(u)], zero_b)

        am = [[[masked(at, u, c, h) for h in range(HPT)] for c in lch] for u in uidx]
        vm = [[[masked(vb, u, c, h) for h in range(HPT)] for c in lch] for u in uidx]

        def score(u, c, h):
            b_, k_ = bt[urows(u, c), usl(u)], kt[urows(u, c), usl(u)]
            rhs = jnp.concatenate([b_, k_] if h == 0 else [k_, b_], axis=0)
            return _dot_nt(jnp.concatenate([am[u][c][h], masked(rt, u, c, h)], axis=0), rhs)

        s = [[[score(u, c, h) for h in range(HPT)] for c in lch] for u in uidx]

        def pick(u, c, part, first):
            e, o = s[u][c][0][part * L:(part + 1) * L], s[u][c][1][part * L:(part + 1) * L]
            return jnp.where(low_half, e, o) if first == 0 else jnp.where(low_half, o, e)

        n_p = [jnp.concatenate([jnp.where(strict, pick(u, c, 0, 0), 0.0) for c in lch], axis=1) for u in uidx]
        ak_p = [jnp.concatenate([bf(jnp.where(strict, pick(u, c, 0, 1), 0.0)) for c in lch], axis=1) for u in uidx]
        rb = [[bf(jnp.where(lower, pick(u, c, 1, 0), 0.0)) for c in lch] for u in uidx]
        rk = [[bf(jnp.where(lower, pick(u, c, 1, 1), 0.0)) for c in lch] for u in uidx]
        eye_p = jnp.where(lrp == rrp, 1.0, 0.0)
        tinv = [eye_p + jnp.where(same_blk(2), n, 0.0) for n in n_p]
        size = 2
        while size < L:
            sub_diag = same_blk(2 * size) & jnp.logical_not(same_blk(size))
            e_bd = [block_diag(bf(jnp.where(sub_diag, n, 0.0))) for n in n_p]
            tb = [bf(ti) for ti in tinv]
            te = [bf(_dot(tb_, e_)) for tb_, e_ in zip(tb, e_bd)]
            tinv = [ti + _dot(te_, block_diag(tb_)) for ti, te_, tb_ in zip(tinv, te, tb)]
            size *= 2
        tb = [bf(ti) for ti in tinv]

        def place(x, c):
            return jnp.concatenate([x if cc == c else zeros_lt for cc in lch], axis=1)

        w = [_dot(ak_p[u], jnp.concatenate([place(vm[u][c][h], c) for c in lch for h in (1, 0)], axis=0))
             for u in uidx]

        def solve(u, c):
            wc = bf(w[u][:, c * LANES:(c + 1) * LANES])
            rhs = jnp.concatenate([jnp.concatenate([am[u][c][h], jnp.where(head_mask[h], wc, zero_b)], axis=1)
                                   for h in range(HPT)], axis=0)
            return _dot(tb[u][:, c * LANES:(c + 1) * LANES], rhs)

        xs = [[solve(u, c) for c in lch] for u in uidx]

        def readout(u, c):
            xb = bf(xs[u][c])
            rhs = jnp.concatenate([jnp.where(head_mask2[0], xb, zero_b), jnp.where(head_mask2[1], xb, zero_b),
                                   jnp.concatenate([zeros_lt, vm[u][c][1]], axis=1),
                                   jnp.concatenate([zeros_lt, vm[u][c][0]], axis=1)], axis=0)
            return _dot(jnp.concatenate([rb[u][c], rk[u][c]], axis=1), rhs)

        zs = [[readout(u, c) for c in lch] for u in uidx]
        x = [[xs[t * ppt + c // CPP][c % CPP] for c in chunks] for t in tiles]
        z = [[zs[t * ppt + c // CPP][c % CPP] for c in chunks] for t in tiles]

        h = [h_ref[t0 + t] for t in tiles]
        y_rows = [[None] * NCH for _ in tiles]
        for c in chunks:
            dec_mid = jnp.exp(cum[c * L + L // 2 - 1:c * L + L // 2, :])
            dec = jnp.exp(cum[c * L + L - 1:c * L + L, :])
            h_mid = [bf(h[t] * jnp.transpose(jnp.broadcast_to(dec_mid[:, tile_sl[t]], (LANES, LANES)))) for t in tiles]
            lhs = [bf(jnp.concatenate([x[t][c][:, :LANES],
                                       rt[rows[c], tile_sl[t]].astype(F32) + z[t][c][:, :LANES]], axis=0)) for t in tiles]
            ah = [_dot(lhs[t], h_mid[t]) for t in tiles]
            u = [ah[t][:L] + x[t][c][:, LANES:] for t in tiles]
            for t in tiles:
                y_rows[t][c] = ah[t][L:] + z[t][c][:, LANES:]
            upd = [_dot_tn(jnp.concatenate([bh[rows[c], tile_sl[t]], kh[rows[c], tile_sl[t]]], axis=0),
                           jnp.concatenate([bf(u[t]), vb[rows[c], tile_sl[t]]], axis=0)) for t in tiles]
            h = [h[t] * jnp.transpose(jnp.broadcast_to(dec[:, tile_sl[t]], (LANES, LANES)))
                 + jnp.where(same_head, upd[t], 0.0) for t in tiles]
        for t in tiles:
            h_ref[t0 + t] = h[t]
        return [jnp.concatenate(y_rows[t], axis=0) for t in tiles]

    n_tiles = C // LANES
    y_tiles = []
    for t0 in range(0, n_tiles, WKV_TILE_GROUP):
        y_tiles += run_tiles(t0, min(WKV_TILE_GROUP, n_tiles - t0))
    y = jnp.concatenate(y_tiles, axis=1)
    ones_bd = _group_ones(RWKV_HEAD)
    inv_n = 1.0 / RWKV_HEAD
    mean = _seg_sum(y, ones_bd) * inv_n
    d = y - mean
    var = _seg_sum(d * d, ones_bd) * inv_n
    yn = d * lax.rsqrt(var + LNX_EPS) * lnw_ref[...] + lnb_ref[...]
    bonus = _seg_sum(r * k * rk_ref[...], ones_bd) * v
    o_ref[...] = (yn + bonus) * g


def _rwkv_mix(p, col0, mu, wwa, g_up, w0, a0, k_k, k_a, ln_w, ln_b, r_k, batch, seq_len, n_decay):
    m = p.shape[0]
    nb = mu.shape[1]
    c = w0.shape[1]
    lw_ = g_up.shape[0]
    assert col0 % c == 0 and (col0 + 3 * c) % lw_ == 0 and nb == 3 * c + lw_
    cb, lb = col0 // c, (col0 + 3 * c) // lw_
    tb = min(WKV_STEP, seq_len)
    assert LANES == 2 * RWKV_HEAD and RWKV_HEAD & (RWKV_HEAD - 1) == 0, "kernel packs two power-of-two heads per lane tile"
    assert tb % (2 * RWKV_HEAD) == 0 and seq_len % tb == 0 and c % LANES == 0
    nc = seq_len // tb
    vec = lambda n: pl.BlockSpec((1, n), lambda bi, ci: (0, 0))
    full = lambda a: pl.BlockSpec(a.shape, lambda bi, ci: (0, 0))
    return pl.pallas_call(
        functools.partial(_wkv_kernel, n_decay),
        grid=(batch, nc),
        in_specs=[pl.BlockSpec((tb, c), lambda bi, ci: (bi * nc + ci, cb)),
                  pl.BlockSpec((tb, c), lambda bi, ci: (bi * nc + ci, cb + 1)),
                  pl.BlockSpec((tb, c), lambda bi, ci: (bi * nc + ci, cb + 2)),
                  pl.BlockSpec((tb, lw_), lambda bi, ci: (bi * nc + ci, lb)),
                  vec(nb), full(wwa), full(g_up)] + [vec(c)] * 7,
        out_specs=pl.BlockSpec((tb, c), lambda bi, ci: (bi * nc + ci, 0)),
        out_shape=jax.ShapeDtypeStruct((m, c), F32),
        scratch_shapes=[pltpu.VMEM((c // LANES, LANES, LANES), F32), pltpu.VMEM((SUBLANES, nb), F32)],
        compiler_params=_params("parallel", "arbitrary"),
        name="rwkv7_mix",
    )(p, p, p, p, mu, wwa, g_up, w0, a0, k_k, k_a, ln_w, ln_b, r_k)


def _out_proj_kernel(phase, ya_ref, yb_ref, yc_ref, w_ref, g_ref, x_ref, o_ref, wb_ref):
    @pl.when(pl.program_id(0) == 0)
    def _():
        wb_ref[...] = w_ref[...].astype(BF16)

    ra = ya_ref.shape[1]
    rc = ra + yb_ref.shape[1]
    rows, d = wb_ref.shape
    start = rc - phase
    full = (rows - start) // LANES * LANES
    yc = yc_ref[...].astype(BF16)
    w_tail = jnp.concatenate([wb_ref[start + full:rows, :],
                              jnp.zeros((LANES - (rows - start - full), d), BF16)], axis=0)
    y = (_dot(ya_ref[...].astype(BF16), wb_ref[0:ra, :]) + _dot(yb_ref[...].astype(BF16), wb_ref[ra:rc, :])
         + _dot(yc[:, :full], wb_ref[start:start + full, :]) + _dot(yc[:, full:], w_tail))
    o_ref[...] = x_ref[...] + _rms(y, g_ref[...])


def _out_proj(ya, yb, yc, phase, w_all, layer, gain, x):
    m, d = x.shape
    rows = w_all.shape[1]
    tm = min(256, m)
    start = ya.shape[1] + yb.shape[1] - phase
    assert yc.shape[1] == (rows - start) // LANES * LANES + LANES and 0 < (rows - start) % LANES
    act = lambda a: pl.BlockSpec((tm, a.shape[1]), lambda i: (i, 0))
    return pl.pallas_call(
        functools.partial(_out_proj_kernel, phase),
        grid=(m // tm,),
        in_specs=[act(ya), act(yb), act(yc),
                  pl.BlockSpec((None, rows, d), lambda i: (layer, 0, 0), pipeline_mode=pl.Buffered(1)),
                  pl.BlockSpec((1, d), lambda i: (0, 0)), act(x)],
        out_specs=act(x),
        out_shape=jax.ShapeDtypeStruct((m, d), F32),
        scratch_shapes=[pltpu.VMEM((rows, d), BF16)],
        compiler_params=_params("arbitrary"),
        name="out_proj",
    )(ya, yb, yc, w_all, gain, x)


def _mlp_kernel(x_ref, gpre_ref, wu_ref, wd_ref, gpost_ref, o_ref, h_ref):
    j = pl.program_id(1)

    @pl.when(j == 0)
    def _():
        h_ref[...] = _rms(x_ref[...], gpre_ref[...]).astype(BF16)
        o_ref[...] = jnp.zeros_like(o_ref)

    f = jnp.maximum(_dot(h_ref[...], wu_ref[...].astype(BF16)), 0.0)
    o_ref[...] += _dot((f * f).astype(BF16), wd_ref[...].astype(BF16))

    @pl.when(j == pl.num_programs(1) - 1)
    def _():
        o_ref[...] = x_ref[...] + _rms(o_ref[...], gpost_ref[...])


def _mlp(x, g_pre, w_up_all, w_down_all, g_post, layer):
    m, d = x.shape
    ff = w_up_all.shape[2]
    tm = min(1024, m)
    tf = min(512, ff)
    return pl.pallas_call(
        _mlp_kernel,
        grid=(m // tm, ff // tf),
        in_specs=[
            pl.BlockSpec((tm, d), lambda i, j: (i, 0)),
            pl.BlockSpec((1, d), lambda i, j: (0, 0)),
            pl.BlockSpec((None, d, tf), lambda i, j: (layer, 0, j)),
            pl.BlockSpec((None, tf, d), lambda i, j: (layer, j, 0)),
            pl.BlockSpec((1, d), lambda i, j: (0, 0)),
        ],
        out_specs=pl.BlockSpec((tm, d), lambda i, j: (i, 0)),
        out_shape=jax.ShapeDtypeStruct((m, d), F32),
        scratch_shapes=[pltpu.VMEM((tm, d), BF16)],
        compiler_params=_params("parallel", "arbitrary", vmem=VMEM_LIMIT_MLP),
        name="relu2_mlp",
    )(x, g_pre, w_up_all, w_down_all, g_post)


def kernel(x, norm_mix_pre, norm_mix_post, norm_mlp_pre, norm_mlp_post, w_in, gm_v_gain, gm_ws, gm_bs,
           gm_out_gain, rk_mu, rk_w0, rk_w_up, rk_a0, rk_a_up, rk_g_up, rk_k_k, rk_k_a, rk_r_k, rk_ln_w,
           rk_ln_b, sc_conv, sc_out_gain, w_out, mlp_up, mlp_down):
    batch, seq_len, d = x.shape
    depth = w_in.shape[0]
    mix_a = gm_v_gain.shape[1]
    mix_b = rk_w0.shape[1]
    mix_c = sc_out_gain.shape[1]
    n_decay, n_iclr, n_gate = rk_w_up.shape[1], rk_a_up.shape[1], rk_g_up.shape[1]
    n_lora = n_decay + n_iclr
    lora_pad = -(-n_lora // LANES) * LANES
    col_b = 2 * mix_a
    col_lora = col_b + 3 * mix_b
    col_gate = col_lora + n_lora
    col_c = col_gate + n_gate
    row = lambda a: a.reshape(1, -1)

    xf = x.reshape(batch * seq_len, d)
    n_in = w_in.shape[2]
    n_whole = n_in // PROJ_TILE_N * PROJ_TILE_N
    assert n_whole < n_in
    w_proj = w_in.astype(BF16)
    w_tail = jnp.pad(w_proj[:, :, n_whole:], ((0, 0), (0, 0), (0, n_whole + PROJ_TILE_N - n_in)))
    lora_blk = -(-(n_lora + n_gate) // LANES) * LANES
    conv_phase = col_c % LANES
    for l in range(depth):
        mu_b = jnp.pad(rk_mu[l], (0, 3 * mix_b + lora_blk - rk_mu.shape[1])).reshape(1, -1)
        wwa = jnp.zeros((lora_pad, 2 * mix_b), F32)
        wwa = wwa.at[:n_decay, :mix_b].set(rk_w_up[l]).at[n_decay:n_lora, mix_b:].set(rk_a_up[l]).astype(BF16)
        g_up = jnp.zeros((lora_blk, mix_b), F32).at[n_lora:n_lora + n_gate].set(rk_g_up[l]).astype(BF16)

        p = _norm_matmul(xf, row(norm_mix_pre[l]), w_proj, w_tail, l)

        ya = _gmlp(p, 0, row(gm_v_gain[l]), gm_ws[l], gm_bs[l].T, row(gm_out_gain[l]))
        yb = _rwkv_mix(p, col_b, mu_b, wwa, g_up, row(rk_w0[l]), row(rk_a0[l]), row(rk_k_k[l]),
                       row(rk_k_a[l]), row(rk_ln_w[l]), row(rk_ln_b[l]), row(rk_r_k[l]), batch, seq_len, n_decay)
        yc = _short_conv(p, col_c, sc_conv[l], row(sc_out_gain[l]), seq_len)

        xf = _out_proj(ya, yb, yc, conv_phase, w_out, l, row(norm_mix_post[l]), xf)
        xf = _mlp(xf, row(norm_mlp_pre[l]), mlp_up, mlp_down, row(norm_mlp_post[l]), l)
    return xf.reshape(batch, seq_len, d)
```

```python
import functools

import jax
import jax.numpy as jnp
from jax import lax
from jax.experimental import pallas as pl
from jax.experimental.pallas import tpu as pltpu

F32 = jnp.float32
BF16 = jnp.bfloat16

RMS_EPS = 1e-6
LNX_EPS = 64e-5
RWKV_HEAD = 64
CONV_GROUP_W = 64
LANES = 128
SUBLANES = 8
WKV_STEP = 128
PROJ_TILE_N = 1024
WKV_TILE_GROUP = 8
V7X_VMEM_BYTES = 64 * 1024 * 1024
VMEM_LIMIT = 56 * 1024 * 1024
VMEM_LIMIT_MLP = V7X_VMEM_BYTES - 2 * 1024 * 1024


def _params(*sem, vmem=VMEM_LIMIT):
    return pltpu.CompilerParams(dimension_semantics=sem, vmem_limit_bytes=vmem)


def _rms(x, gain):
    return x * lax.rsqrt(jnp.mean(x * x, axis=-1, keepdims=True) + RMS_EPS) * gain


def _dot(a, b):
    return jnp.dot(a, b, preferred_element_type=F32)


def _dot_exact_lhs(a, b):
    b0 = b.astype(BF16)
    r1 = b - b0.astype(F32)
    b1 = r1.astype(BF16)
    b2 = (r1 - b1.astype(F32)).astype(BF16)
    ab = a.astype(BF16)
    return _dot(ab, b0) + (_dot(ab, b1) + _dot(ab, b2))


def _dot_nt(a, b):
    return lax.dot_general(a, b, (((1,), (1,)), ((), ())), preferred_element_type=F32)


def _dot_tn(a, b):
    return lax.dot_general(a, b, (((0,), (0,)), ((), ())), preferred_element_type=F32)


def _group_ones(width):
    i = lax.broadcasted_iota(jnp.int32, (LANES, LANES), 0) // width
    j = lax.broadcasted_iota(jnp.int32, (LANES, LANES), 1) // width
    return jnp.where(i == j, 1.0, 0.0).astype(BF16)


def _seg_sum(x, ones_bd):
    xb = x.astype(BF16)
    cols = [_dot(xb[:, c * LANES:(c + 1) * LANES], ones_bd) for c in range(x.shape[-1] // LANES)]
    return cols[0] if len(cols) == 1 else jnp.concatenate(cols, axis=-1)


def _shift_rows(x, prev_rows, n):
    rolled = pltpu.roll(x, n, axis=0)
    row = lax.broadcasted_iota(jnp.int32, x.shape, 0)
    p = prev_rows.shape[0]
    for j in range(n):
        rolled = jnp.where(row == j, prev_rows[p - n + j:p - n + j + 1, :], rolled)
    return rolled


def _norm_matmul_kernel(n_full, x_ref, g_ref, w_ref, wt_ref, o_ref, h_ref):
    j = pl.program_id(1)

    @pl.when(j == 0)
    def _():
        h_ref[...] = _rms(x_ref[...], g_ref[...]).astype(BF16)

    @pl.when(j < n_full)
    def _():
        o_ref[...] = _dot(h_ref[...], w_ref[...])

    @pl.when(j >= n_full)
    def _():
        o_ref[...] = _dot(h_ref[...], wt_ref[...])


def _norm_matmul(x, gain, w_all, w_tail, layer):
    m, d = x.shape
    tm = min(1024, m)
    tn = w_tail.shape[2]
    n_full = w_all.shape[2] // tn
    return pl.pallas_call(
        functools.partial(_norm_matmul_kernel, n_full),
        grid=(m // tm, n_full + 1),
        in_specs=[
            pl.BlockSpec((tm, d), lambda i, j: (i, 0)),
            pl.BlockSpec((1, d), lambda i, j: (0, 0)),
            pl.BlockSpec((None, d, tn), lambda i, j: (layer, 0, jnp.minimum(j, n_full - 1))),
            pl.BlockSpec((None, d, tn), lambda i, j: (layer, 0, 0), pipeline_mode=pl.Buffered(1)),
        ],
        out_specs=pl.BlockSpec((tm, tn), lambda i, j: (i, j)),
        out_shape=jax.ShapeDtypeStruct((m, (n_full + 1) * tn), F32),
        scratch_shapes=[pltpu.VMEM((tm, d), BF16)],
        compiler_params=_params("parallel", "arbitrary"),
        name="in_proj",
    )(x, gain, w_all, w_tail)


def _gelu_tanh(x):
    return 0.5 * x * (1.0 + jnp.tanh(0.7978845608028654 * (x + 0.044715 * (x * x * x))))


def _gmlp_kernel(pu_ref, pv_ref, vg_ref, ws_ref, bst_ref, og_ref, o_ref):
    groups, chunk, _ = ws_ref.shape
    tm = pu_ref.shape[0]
    zu = _gelu_tanh(pu_ref[...])
    zv = _gelu_tanh(pv_ref[...])
    row = lax.broadcasted_iota(jnp.int32, (chunk, chunk), 0)
    col = lax.broadcasted_iota(jnp.int32, (chunk, chunk), 1)
    bst = bst_ref[...]
    for g in range(groups):
        sl = slice(g * LANES, (g + 1) * LANES)
        u = zu[:, sl]
        v = _rms(zv[:, sl], vg_ref[:, sl]).astype(BF16)
        w = jnp.where(col <= row, ws_ref[g], 0.0).astype(BF16)
        bias = bst[:, g:g + 1]
        for c in range(tm // chunk):
            rows = slice(c * chunk, (c + 1) * chunk)
            mixed = _dot(w, v[rows]) + bias
            o_ref[rows, sl] = _rms(u[rows] * mixed, og_ref[:, sl])


def _gmlp(p, col0, v_gain, ws, bs_t, out_gain):
    m = p.shape[0]
    groups, chunk, _ = ws.shape
    half = v_gain.shape[1]
    cb = col0 // half
    assert cb * half == col0
    tm = min(512, m)
    return pl.pallas_call(
        _gmlp_kernel,
        grid=(m // tm,),
        in_specs=[
            pl.BlockSpec((tm, half), lambda i: (i, cb)),
            pl.BlockSpec((tm, half), lambda i: (i, cb + 1)),
            pl.BlockSpec((1, half), lambda i: (0, 0)),
            pl.BlockSpec((groups, chunk, chunk), lambda i: (0, 0, 0)),
            pl.BlockSpec((chunk, groups), lambda i: (0, 0)),
            pl.BlockSpec((1, half), lambda i: (0, 0)),
        ],
        out_specs=pl.BlockSpec((tm, half), lambda i: (i, 0)),
        out_shape=jax.ShapeDtypeStruct((m, half), F32),
        compiler_params=_params("parallel"),
        name="gmlp_mix",
    )(p, p, v_gain, ws, bs_t, out_gain)


def _conv_kernel(seq_len, starts, p_ref, prev_ref, cw_ref, og_ref, o_ref):
    tm = p_ref.shape[0]
    fw = o_ref.shape[1]
    sg, sc, sh = starts
    p = p_ref[...]
    gb, z = p[:, sg:sg + fw], p[:, sc:sc + fw] * p[:, sh:sh + fw]
    pv = prev_ref[...]
    seq_start = (pl.program_id(0) * tm) % seq_len == 0
    zp = jnp.where(seq_start, 0.0, pv[:, sc:sc + fw] * pv[:, sh:sh + fw])
    cw = cw_ref[...]
    taps = cw.shape[0]
    zc = cw[taps - 1:taps, :] * z
    for s in range(1, taps):
        zc = zc + cw[taps - 1 - s:taps - s, :] * _shift_rows(z, zp, s)
    y = gb * zc
    ms = _seg_sum(y * y, _group_ones(CONV_GROUP_W)) * (1.0 / CONV_GROUP_W)
    o_ref[...] = y * lax.rsqrt(ms + RMS_EPS) * og_ref[...]


def _short_conv(p, col0, conv_w, out_gain, seq_len):
    m, n_all = p.shape
    c = out_gain.shape[1]
    phase = col0 % LANES
    fw = c + LANES
    win = 4 * c
    wb = col0 // win
    assert (wb + 1) * win <= n_all and c % LANES == 0 and phase % CONV_GROUP_W == 0
    starts = tuple(col0 + j * c - phase - wb * win for j in range(3))
    assert starts[0] >= 0 and starts[2] + fw <= win
    pad = lambda a: jnp.pad(a, ((0, 0), (phase, LANES - phase)))
    tm = min(512, seq_len)
    assert seq_len % tm == 0, "a row block must not straddle two sequences"
    rb = tm // SUBLANES
    return pl.pallas_call(
        functools.partial(_conv_kernel, seq_len, starts),
        grid=(m // tm,),
        in_specs=[
            pl.BlockSpec((tm, win), lambda i: (i, wb)),
            pl.BlockSpec((SUBLANES, win), lambda i: (jnp.maximum(i * rb - 1, 0), wb)),
            pl.BlockSpec((conv_w.shape[0], fw), lambda i: (0, 0)),
            pl.BlockSpec((1, fw), lambda i: (0, 0)),
        ],
        out_specs=pl.BlockSpec((tm, fw), lambda i: (i, 0)),
        out_shape=jax.ShapeDtypeStruct((m, fw), F32),
        compiler_params=_params("parallel"),
        name="short_conv_mix",
    )(p, p, pad(conv_w), pad(out_gain))


def _local_mix_kernel(seq_len, starts, pu_ref, pv_ref, vg_ref, ws_ref, bst_ref, oga_ref, pw_ref, prev_ref, cw_ref,
                      ogc_ref, ya_ref, yc_ref):
    _gmlp_kernel(pu_ref, pv_ref, vg_ref, ws_ref, bst_ref, oga_ref, ya_ref)
    _conv_kernel(seq_len, starts, pw_ref, prev_ref, cw_ref, ogc_ref, yc_ref)


def _local_mix(p, col_a, v_gain, ws, bs_t, gain_a, col_c, conv_w, gain_c, seq_len):
    m, n_all = p.shape
    groups, chunk, _ = ws.shape
    half = v_gain.shape[1]
    ab = col_a // half
    c = gain_c.shape[1]
    phase = col_c % LANES
    fw = c + LANES
    win = 4 * c
    wb = col_c // win
    starts = tuple(col_c + j * c - phase - wb * win for j in range(3))
    assert ab * half == col_a and (wb + 1) * win <= n_all and c % LANES == 0 and phase % CONV_GROUP_W == 0
    assert starts[0] >= 0 and starts[2] + fw <= win
    pad = lambda a: jnp.pad(a, ((0, 0), (phase, LANES - phase)))
    tm = min(512, seq_len)
    assert seq_len % tm == 0 and tm % chunk == 0, "a row block must not straddle two sequences or split a chunk"
    rb = tm // SUBLANES
    const = lambda shape: pl.BlockSpec(shape, lambda i: (0,) * len(shape))
    return pl.pallas_call(
        functools.partial(_local_mix_kernel, seq_len, starts),
        grid=(m // tm,),
        in_specs=[
            pl.BlockSpec((tm, half), lambda i: (i, ab)),
            pl.BlockSpec((tm, half), lambda i: (i, ab + 1)),
            const((1, half)), const((groups, chunk, chunk)), const((chunk, groups)), const((1, half)),
            pl.BlockSpec((tm, win), lambda i: (i, wb)),
            pl.BlockSpec((SUBLANES, win), lambda i: (jnp.maximum(i * rb - 1, 0), wb)),
            const((conv_w.shape[0], fw)), const((1, fw)),
        ],
        out_specs=[pl.BlockSpec((tm, half), lambda i: (i, 0)), pl.BlockSpec((tm, fw), lambda i: (i, 0))],
        out_shape=[jax.ShapeDtypeStruct((m, half), F32), jax.ShapeDtypeStruct((m, fw), F32)],
        compiler_params=_params("parallel"),
        name="gmlp_conv_mix",
    )(p, p, v_gain, ws, bs_t, gain_a, p, p, pad(conv_w), pad(gain_c))


def _rwkv_operands(p, prev_row, mu, wwa, g_up, w0, a0, k_k, k_a, n_decay):
    c = w0.shape[1]
    ps = p + (_shift_rows(p, prev_row, 1) - p) * mu
    r, k, v = ps[:, :c], ps[:, c:2 * c], ps[:, 2 * c:3 * c]
    lo = ps[:, 3 * c:3 * c + wwa.shape[0]]
    gd = ps[:, 3 * c:]
    lane = lax.broadcasted_iota(jnp.int32, lo.shape, 1)
    act = jnp.where(lane < n_decay, jnp.tanh(lo), lo)
    wa = _dot(act.astype(BF16), wwa)
    zw = w0 + wa[:, :c]
    w = -(jnp.maximum(-zw, 0.0) + jnp.log(1.0 + jnp.exp(-jnp.abs(zw)))) - 0.5
    iclr = 1.0 / (1.0 + jnp.exp(-(a0 + wa[:, c:])))
    g = _dot((1.0 / (1.0 + jnp.exp(-gd))).astype(BF16), g_up)
    kkf = k * k_k
    nrm = jnp.sqrt(_seg_sum(kkf * kkf, _group_ones(RWKV_HEAD)))
    kk = kkf / jnp.maximum(nrm, 1e-12)
    return r, k * (1.0 + (iclr - 1.0) * k_a), v, -jnp.exp(w), -kk, kk * iclr, g


def _wkv_kernel(n_decay, pr_ref, pk_ref, pv_ref, pl_ref, mu_ref, wwa_ref, gup_ref, w0_ref, a0_ref, kk_ref, ka_ref,
                lnw_ref, lnb_ref, rk_ref, o_ref, h_ref, prev_ref):
    @pl.when(pl.program_id(1) == 0)
    def _():
        h_ref[...] = jnp.zeros_like(h_ref)
        prev_ref[...] = jnp.zeros_like(prev_ref)

    TB, C = o_ref.shape
    L = RWKV_HEAD
    NCH = TB // L
    HPT = LANES // RWKV_HEAD
    CPP = 2
    NB = CPP * HPT
    p = jnp.concatenate([pr_ref[...], pk_ref[...], pv_ref[...], pl_ref[...]], axis=1)
    r, k, v, lw, a, b, g = _rwkv_operands(p, prev_ref[0:1, :], mu_ref[...], wwa_ref[...], gup_ref[...], w0_ref[...],
                                          a0_ref[...], kk_ref[...], ka_ref[...], n_decay)
    prev_ref[0:1, :] = p[TB - 1:TB, :]
    row = lax.broadcasted_iota(jnp.int32, (TB, TB), 0)
    col = lax.broadcasted_iota(jnp.int32, (TB, TB), 1)
    cum = _dot_exact_lhs(jnp.where((col <= row) & (row // L == col // L), 1.0, 0.0), lw)

    def chunk_rows(i):
        return jnp.concatenate([jnp.broadcast_to(cum[c * L + i:c * L + i + 1], (L, C)) for c in range(NCH)], axis=0)

    c_mid = chunk_rows(L // 2 - 1)
    c_end = chunk_rows(L - 1)
    e_fwd = jnp.exp(cum - c_mid)
    e_bwd = jnp.exp(c_mid - cum)
    e_end = jnp.exp(c_end - cum)
    bf = lambda z: z.astype(BF16)
    rt = bf(r * e_fwd)
    at = bf(a * jnp.exp(cum - lw - c_mid))
    bt = bf(b * e_bwd)
    kt = bf(k * e_bwd)
    bh = bf(b * e_end)
    kh = bf(k * e_end)
    vb = bf(v)

    lane = lax.broadcasted_iota(jnp.int32, (1, LANES), 1)
    hi = lax.broadcasted_iota(jnp.int32, (LANES, LANES), 0) // RWKV_HEAD
    hj = lax.broadcasted_iota(jnp.int32, (LANES, LANES), 1) // RWKV_HEAD
    same_head = hi == hj
    zero_b = jnp.zeros((), BF16)
    zeros_lt = jnp.zeros((L, LANES), BF16)

    chunks = range(NCH)
    rows = [slice(c * L, (c + 1) * L) for c in chunks]
    head_mask = [(lane // RWKV_HEAD) == h for h in range(HPT)]
    head_mask2 = [jnp.concatenate([m, m], axis=1) for m in head_mask]
    low_half = lane < RWKV_HEAD
    lr = lax.broadcasted_iota(jnp.int32, (L, LANES), 1) % L
    rr = lax.broadcasted_iota(jnp.int32, (L, LANES), 0)
    strict = lr < rr
    lower = lr <= rr
    lrp = lax.broadcasted_iota(jnp.int32, (L, NB * L), 1) % L
    rrp = lax.broadcasted_iota(jnp.int32, (L, NB * L), 0)
    blkp = lax.broadcasted_iota(jnp.int32, (L, NB * L), 1) // L
    same_blk = lambda size: (lrp // size) == (rrp // size)

    def block_diag(xp):
        return jnp.concatenate([jnp.where(blkp == q, xp, zero_b) for q in range(NB)], axis=0)

    def run_tiles(t0, nt):
        tiles = range(nt)
        tile_sl = [slice((t0 + t) * LANES, (t0 + t + 1) * LANES) for t in tiles]
        ppt = NCH // CPP
        units = [(t, p) for t in tiles for p in range(ppt)]
        uidx = range(len(units))
        lch = range(CPP)
        usl = lambda u: tile_sl[units[u][0]]
        urows = lambda u, c: rows[units[u][1] * CPP + c]

        def masked(x, u, c, h):
            return jnp.where(head_mask[h], x[urows(u, c), usl(u)], zero_b)

        am = [[[masked(at, u, c, h) for h in range(HPT)] for c in lch] for u in uidx]
        vm = [[[masked(vb, u, c, h) for h in range(HPT)] for c in lch] for u in uidx]

        def score(u, c, h):
            b_, k_ = bt[urows(u, c), usl(u)], kt[urows(u, c), usl(u)]
            rhs = jnp.concatenate([b_, k_] if h == 0 else [k_, b_], axis=0)
            return _dot_nt(jnp.concatenate([am[u][c][h], masked(rt, u, c, h)], axis=0), rhs)

        s = [[[score(u, c, h) for h in range(HPT)] for c in lch] for u in uidx]

        def pick(u, c, part, first):
            e, o = s[u][c][0][part * L:(part + 1) * L], s[u][c][1][part * L:(part + 1) * L]
            return jnp.where(low_half, e, o) if first == 0 else jnp.where(low_half, o, e)

        n_p = [jnp.concatenate([jnp.where(strict, pick(u, c, 0, 0), 0.0) for c in lch], axis=1) for u in uidx]
        ak_p = [jnp.concatenate([bf(jnp.where(strict, pick(u, c, 0, 1), 0.0)) for c in lch], axis=1) for u in uidx]
        rb = [[bf(jnp.where(lower, pick(u, c, 1, 0), 0.0)) for c in lch] for u in uidx]
        rk = [[bf(jnp.where(lower, pick(u, c, 1, 1), 0.0)) for c in lch] for u in uidx]
        eye_p = jnp.where(lrp == rrp, 1.0, 0.0)
        tinv = [eye_p + jnp.where(same_blk(2), n, 0.0) for n in n_p]
        size = 2
        while size < L:
            sub_diag = same_blk(2 * size) & jnp.logical_not(same_blk(size))
            e_bd = [block_diag(bf(jnp.where(sub_diag, n, 0.0))) for n in n_p]
            tb = [bf(ti) for ti in tinv]
            te = [bf(_dot(tb_, e_)) for tb_, e_ in zip(tb, e_bd)]
            tinv = [ti + _dot(te_, block_diag(tb_)) for ti, te_, tb_ in zip(tinv, te, tb)]
            size *= 2
        tb = [bf(ti) for ti in tinv]

        def place(x, c):
            return jnp.concatenate([x if cc == c else zeros_lt for cc in lch], axis=1)

        w = [_dot(ak_p[u], jnp.concatenate([place(vm[u][c][h], c) for c in lch for h in (1, 0)], axis=0))
             for u in uidx]

        def solve(u, c):
            wc = bf(w[u][:, c * LANES:(c + 1) * LANES])
            rhs = jnp.concatenate([jnp.concatenate([am[u][c][h], jnp.where(head_mask[h], wc, zero_b)], axis=1)
                                   for h in range(HPT)], axis=0)
            return _dot(tb[u][:, c * LANES:(c + 1) * LANES], rhs)

        xs = [[solve(u, c) for c in lch] for u in uidx]

        def readout(u, c):
            xb = bf(xs[u][c])
            rhs = jnp.concatenate([jnp.where(head_mask2[0], xb, zero_b), jnp.where(head_mask2[1], xb, zero_b),
                                   jnp.concatenate([zeros_lt, vm[u][c][1]], axis=1),
                                   jnp.concatenate([zeros_lt, vm[u][c][0]], axis=1)], axis=0)
            return _dot(jnp.concatenate([rb[u][c], rk[u][c]], axis=1), rhs)

        zs = [[readout(u, c) for c in lch] for u in uidx]
        x = [[xs[t * ppt + c // CPP][c % CPP] for c in chunks] for t in tiles]
        z = [[zs[t * ppt + c // CPP][c % CPP] for c in chunks] for t in tiles]

        h = [h_ref[t0 + t] for t in tiles]
        y_rows = [[None] * NCH for _ in tiles]
        for c in chunks:
            dec_mid = jnp.exp(cum[c * L + L // 2 - 1:c * L + L // 2, :])
            dec = jnp.exp(cum[c * L + L - 1:c * L + L, :])
            h_mid = [bf(h[t] * jnp.transpose(jnp.broadcast_to(dec_mid[:, tile_sl[t]], (LANES, LANES)))) for t in tiles]
            lhs = [bf(jnp.concatenate([x[t][c][:, :LANES],
                                       rt[rows[c], tile_sl[t]].astype(F32) + z[t][c][:, :LANES]], axis=0)) for t in tiles]
            ah = [_dot(lhs[t], h_mid[t]) for t in tiles]
            u = [ah[t][:L] + x[t][c][:, LANES:] for t in tiles]
            for t in tiles:
                y_rows[t][c] = ah[t][L:] + z[t][c][:, LANES:]
            upd = [_dot_tn(jnp.concatenate([bh[rows[c], tile_sl[t]], kh[rows[c], tile_sl[t]]], axis=0),
                           jnp.concatenate([bf(u[t]), vb[rows[c], tile_sl[t]]], axis=0)) for t in tiles]
            h = [h[t] * jnp.transpose(jnp.broadcast_to(dec[:, tile_sl[t]], (LANES, LANES)))
                 + jnp.where(same_head, upd[t], 0.0) for t in tiles]
        for t in tiles:
            h_ref[t0 + t] = h[t]
        return [jnp.concatenate(y_rows[t], axis=0) for t in tiles]

    n_tiles = C // LANES
    y_tiles = []
    for t0 in range(0, n_tiles, WKV_TILE_GROUP):
        y_tiles += run_tiles(t0, min(WKV_TILE_GROUP, n_tiles - t0))
    y = jnp.concatenate(y_tiles, axis=1)
    ones_bd = _group_ones(RWKV_HEAD)
    inv_n = 1.0 / RWKV_HEAD
    mean = _seg_sum(y, ones_bd) * inv_n
    d = y - mean
    var = _seg_sum(d * d, ones_bd) * inv_n
    yn = d * lax.rsqrt(var + LNX_EPS) * lnw_ref[...] + lnb_ref[...]
    bonus = _seg_sum(r * k * rk_ref[...], ones_bd) * v
    o_ref[...] = (yn + bonus) * g


def _rwkv_mix(p, col0, mu, wwa, g_up, w0, a0, k_k, k_a, ln_w, ln_b, r_k, batch, seq_len, n_decay):
    m = p.shape[0]
    nb = mu.shape[1]
    c = w0.shape[1]
    lw_ = g_up.shape[0]
    assert col0 % c == 0 and (col0 + 3 * c) % lw_ == 0 and nb == 3 * c + lw_
    cb, lb = col0 // c, (col0 + 3 * c) // lw_
    tb = min(WKV_STEP, seq_len)
    assert LANES == 2 * RWKV_HEAD and RWKV_HEAD & (RWKV_HEAD - 1) == 0, "kernel packs two power-of-two heads per lane tile"
    assert tb % (2 * RWKV_HEAD) == 0 and seq_len % tb == 0 and c % LANES == 0
    nc = seq_len // tb
    vec = lambda n: pl.BlockSpec((1, n), lambda bi, ci: (0, 0))
    full = lambda a: pl.BlockSpec(a.shape, lambda bi, ci: (0, 0))
    return pl.pallas_call(
        functools.partial(_wkv_kernel, n_decay),
        grid=(batch, nc),
        in_specs=[pl.BlockSpec((tb, c), lambda bi, ci: (bi * nc + ci, cb)),
                  pl.BlockSpec((tb, c), lambda bi, ci: (bi * nc + ci, cb + 1)),
                  pl.BlockSpec((tb, c), lambda bi, ci: (bi * nc + ci, cb + 2)),
                  pl.BlockSpec((tb, lw_), lambda bi, ci: (bi * nc + ci, lb)),
                  vec(nb), full(wwa), full(g_up)] + [vec(c)] * 7,
        out_specs=pl.BlockSpec((tb, c), lambda bi, ci: (bi * nc + ci, 0)),
        out_shape=jax.ShapeDtypeStruct((m, c), F32),
        scratch_shapes=[pltpu.VMEM((c // LANES, LANES, LANES), F32), pltpu.VMEM((SUBLANES, nb), F32)],
        compiler_params=_params("parallel", "arbitrary"),
        name="rwkv7_mix",
    )(p, p, p, p, mu, wwa, g_up, w0, a0, k_k, k_a, ln_w, ln_b, r_k)


def _out_proj_kernel(phase, ya_ref, yb_ref, yc_ref, w_ref, g_ref, x_ref, o_ref, wb_ref):
    @pl.when(pl.program_id(0) == 0)
    def _():
        wb_ref[...] = w_ref[...].astype(BF16)

    ra = ya_ref.shape[1]
    rc = ra + yb_ref.shape[1]
    rows, d = wb_ref.shape
    start = rc - phase
    full = (rows - start) // LANES * LANES
    yc = yc_ref[...].astype(BF16)
    w_tail = jnp.concatenate([wb_ref[start + full:rows, :],
                              jnp.zeros((LANES - (rows - start - full), d), BF16)], axis=0)
    y = (_dot(ya_ref[...].astype(BF16), wb_ref[0:ra, :]) + _dot(yb_ref[...].astype(BF16), wb_ref[ra:rc, :])
         + _dot(yc[:, :full], wb_ref[start:start + full, :]) + _dot(yc[:, full:], w_tail))
    o_ref[...] = x_ref[...] + _rms(y, g_ref[...])


def _out_proj(ya, yb, yc, phase, w_all, layer, gain, x):
    m, d = x.shape
    rows = w_all.shape[1]
    tm = min(256, m)
    start = ya.shape[1] + yb.shape[1] - phase
    assert yc.shape[1] == (rows - start) // LANES * LANES + LANES and 0 < (rows - start) % LANES
    act = lambda a: pl.BlockSpec((tm, a.shape[1]), lambda i: (i, 0))
    return pl.pallas_call(
        functools.partial(_out_proj_kernel, phase),
        grid=(m // tm,),
        in_specs=[act(ya), act(yb), act(yc),
                  pl.BlockSpec((None, rows, d), lambda i: (layer, 0, 0), pipeline_mode=pl.Buffered(1)),
                  pl.BlockSpec((1, d), lambda i: (0, 0)), act(x)],
        out_specs=act(x),
        out_shape=jax.ShapeDtypeStruct((m, d), F32),
        scratch_shapes=[pltpu.VMEM((rows, d), BF16)],
        compiler_params=_params("arbitrary"),
        name="out_proj",
    )(ya, yb, yc, w_all, gain, x)


def _mlp_kernel(x_ref, gpre_ref, wu_ref, wd_ref, gpost_ref, o_ref, h_ref):
    j = pl.program_id(1)

    @pl.when(j == 0)
    def _():
        h_ref[...] = _rms(x_ref[...], gpre_ref[...]).astype(BF16)
        o_ref[...] = jnp.zeros_like(o_ref)

    f = jnp.maximum(_dot(h_ref[...], wu_ref[...].astype(BF16)), 0.0)
    o_ref[...] += _dot((f * f).astype(BF16), wd_ref[...].astype(BF16))

    @pl.when(j == pl.num_programs(1) - 1)
    def _():
        o_ref[...] = x_ref[...] + _rms(o_ref[...], gpost_ref[...])


def _mlp(x, g_pre, w_up_all, w_down_all, g_post, layer):
    m, d = x.shape
    ff = w_up_all.shape[2]
    tm = min(1024, m)
    tf = min(512, ff)
    return pl.pallas_call(
        _mlp_kernel,
        grid=(m // tm, ff // tf),
        in_specs=[
            pl.BlockSpec((tm, d), lambda i, j: (i, 0)),
            pl.BlockSpec((1, d), lambda i, j: (0, 0)),
            pl.BlockSpec((None, d, tf), lambda i, j: (layer, 0, j)),
            pl.BlockSpec((None, tf, d), lambda i, j: (layer, j, 0)),
            pl.BlockSpec((1, d), lambda i, j: (0, 0)),
        ],
        out_specs=pl.BlockSpec((tm, d), lambda i, j: (i, 0)),
        out_shape=jax.ShapeDtypeStruct((m, d), F32),
        scratch_shapes=[pltpu.VMEM((tm, d), BF16)],
        compiler_params=_params("parallel", "arbitrary", vmem=VMEM_LIMIT_MLP),
        name="relu2_mlp",
    )(x, g_pre, w_up_all, w_down_all, g_post)


def kernel(x, norm_mix_pre, norm_mix_post, norm_mlp_pre, norm_mlp_post, w_in, gm_v_gain, gm_ws, gm_bs,
           gm_out_gain, rk_mu, rk_w0, rk_w_up, rk_a0, rk_a_up, rk_g_up, rk_k_k, rk_k_a, rk_r_k, rk_ln_w,
           rk_ln_b, sc_conv, sc_out_gain, w_out, mlp_up, mlp_down):
    batch, seq_len, d = x.shape
    depth = w_in.shape[0]
    mix_a = gm_v_gain.shape[1]
    mix_b = rk_w0.shape[1]
    mix_c = sc_out_gain.shape[1]
    n_decay, n_iclr, n_gate = rk_w_up.shape[1], rk_a_up.shape[1], rk_g_up.shape[1]
    n_lora = n_decay + n_iclr
    lora_pad = -(-n_lora // LANES) * LANES
    col_b = 2 * mix_a
    col_lora = col_b + 3 * mix_b
    col_gate = col_lora + n_lora
    col_c = col_gate + n_gate
    row = lambda a: a.reshape(1, -1)

    xf = x.reshape(batch * seq_len, d)
    n_in = w_in.shape[2]
    n_whole = n_in // PROJ_TILE_N * PROJ_TILE_N
    assert n_whole < n_in
    w_proj = w_in.astype(BF16)
    w_tail = jnp.pad(w_proj[:, :, n_whole:], ((0, 0), (0, 0), (0, n_whole + PROJ_TILE_N - n_in)))
    lora_blk = -(-(n_lora + n_gate) // LANES) * LANES
    conv_phase = col_c % LANES
    for l in range(depth):
        mu_b = jnp.pad(rk_mu[l], (0, 3 * mix_b + lora_blk - rk_mu.shape[1])).reshape(1, -1)
        wwa = jnp.zeros((lora_pad, 2 * mix_b), F32)
        wwa = wwa.at[:n_decay, :mix_b].set(rk_w_up[l]).at[n_decay:n_lora, mix_b:].set(rk_a_up[l]).astype(BF16)
        g_up = jnp.zeros((lora_blk, mix_b), F32).at[n_lora:n_lora + n_gate].set(rk_g_up[l]).astype(BF16)

        p = _norm_matmul(xf, row(norm_mix_pre[l]), w_proj, w_tail, l)

        ya, yc = _local_mix(p, 0, row(gm_v_gain[l]), gm_ws[l], gm_bs[l].T, row(gm_out_gain[l]),
                            col_c, sc_conv[l], row(sc_out_gain[l]), seq_len)
        yb = _rwkv_mix(p, col_b, mu_b, wwa, g_up, row(rk_w0[l]), row(rk_a0[l]), row(rk_k_k[l]),
                       row(rk_k_a[l]), row(rk_ln_w[l]), row(rk_ln_b[l]), row(rk_r_k[l]), batch, seq_len, n_decay)

        xf = _out_proj(ya, yb, yc, conv_phase, w_out, l, row(norm_mix_post[l]), xf)
        xf = _mlp(xf, row(norm_mlp_pre[l]), mlp_up, mlp_down, row(norm_mlp_post[l]), l)
    return xf.reshape(batch, seq_len, d)
```
